```python
import jax
import jax.numpy as jnp
from jax import lax
import numpy as np

D_MODEL = 1024
BATCH = 16
SEQ = 2048
DEPTH = 2

HEAD_DIM = 64
ROT_DIM = HEAD_DIM // 4
ROPE_THETA = 500000.0
NORM_EPS = 1e-5
ATTN_QBLOCK = 128
A_HEADS = D_MODEL // HEAD_DIM
A_GROUPS = 4
A_QPG = A_HEADS // A_GROUPS
CMP_LEN = 32
CMP_STRIDE = 16
CMP_HID = 2 * HEAD_DIM
SEL_LEN = 64
SEL_TOPN = 16
SEL_QCHUNK = 32
A_WINDOW = 512
A_PROJ = A_HEADS * HEAD_DIM + 6 * A_GROUPS * HEAD_DIM + 3 * A_HEADS
B_HEADS = D_MODEL // HEAD_DIM
B_QPG = 8
B_KV_HEADS = B_HEADS // B_QPG
B_WINDOW = 128
N_EXPERTS = 32
TOP_K = 4
D_FF = D_MODEL
SWIGLU_LIMIT = 7.0
SWIGLU_ALPHA = 1.702
MOE_BLOCK = 256
N_A_LAYERS = DEPTH // 2
N_B_LAYERS = DEPTH - N_A_LAYERS

kernel_name = 'hybrid_nsa_yoco_swa_sink_moe'


def rms_norm(x, g):
    x32 = x.astype(jnp.float32)
    y = x32 * lax.rsqrt(jnp.mean(x32 * x32, axis=-1, keepdims=True) + NORM_EPS)
    return (y * g.astype(jnp.float32)).astype(x.dtype)


def ada_mod(c_act, w, b, n):
    m = c_act @ w + b
    return jnp.split(m[:, None, :], n, axis=-1)


def rope_cos_sin(pos):
    inv = ROPE_THETA ** (-jnp.arange(0, ROT_DIM, 2, dtype=jnp.float32) / ROT_DIM)
    ang = pos.astype(jnp.float32)[..., None] * inv
    return jnp.cos(ang), jnp.sin(ang)


def apply_rope(x, cos, sin):
    shp = cos.shape[:2] + (1,) * (x.ndim - 3) + cos.shape[-1:]
    cos = cos.reshape(shp).astype(x.dtype)
    sin = sin.reshape(shp).astype(x.dtype)
    half = ROT_DIM // 2
    x1 = x[..., :half]
    x2 = x[..., half:ROT_DIM]
    return jnp.concatenate([x1 * cos - x2 * sin, x2 * cos + x1 * sin, x[..., ROT_DIM:]], axis=-1)


def masked_softmax(s, mask, sink=None):
    s = jnp.where(mask, s.astype(jnp.float32), -jnp.inf)
    m = jnp.max(s, axis=-1, keepdims=True)
    if sink is not None:
        sink = sink.astype(jnp.float32)
        m = jnp.maximum(m, sink)
    m = jnp.where(jnp.isfinite(m), m, 0.0)
    p = jnp.exp(s - m)
    den = jnp.sum(p, axis=-1, keepdims=True)
    if sink is not None:
        den = den + jnp.exp(sink - m)
    return p / jnp.maximum(den, jnp.finfo(jnp.float32).tiny)


def banded_attention(q, k, v, window, sinks=None):
    B, S, G, QPG, dh = q.shape
    QB = ATTN_QBLOCK
    nb = S // QB
    span = QB + window
    scale = dh ** -0.5
    pad = ((0, 0), (window, 0), (0, 0), (0, 0))
    kp = jnp.pad(k, pad)
    vp = jnp.pad(v, pad)
    qb = q.reshape(B, nb, QB, G, QPG, dh).swapaxes(0, 1)
    sink = None if sinks is None else sinks.reshape(G, QPG)[None, :, :, None, None]

    def block(args):
        q_blk, bi = args
        s0 = bi * QB
        k_blk = lax.dynamic_slice_in_dim(kp, s0, span, axis=1)
        v_blk = lax.dynamic_slice_in_dim(vp, s0, span, axis=1)
        s = jnp.einsum('bqghd,bkgd->bghqk', q_blk, k_blk) * scale
        qpos = s0 + jnp.arange(QB)
        kpos = s0 - window + jnp.arange(span)
        mask = ((kpos[None, :] <= qpos[:, None]) & (kpos[None, :] > qpos[:, None] - window)
                & (kpos[None, :] >= 0))
        p = masked_softmax(s, mask, sink)
        return jnp.einsum('bghqk,bkgd->bqghd', p.astype(v.dtype), v_blk)

    o = lax.map(block, (qb, jnp.arange(nb)))
    return o.swapaxes(0, 1).reshape(B, S, G, QPG, dh)


def selected_block_attention(q, k, v, sel):
    B, S, G, QPG, dh = q.shape
    n_top = sel.shape[-1]
    n_blk = S // SEL_LEN
    n_chunk = S // SEL_QCHUNK
    scale = dh ** -0.5
    kb = k.transpose(0, 2, 1, 3).reshape(B, G, n_blk, SEL_LEN, dh)
    vb = v.transpose(0, 2, 1, 3).reshape(B, G, n_blk, SEL_LEN, dh)
    qc = q.reshape(B, n_chunk, SEL_QCHUNK, G, QPG, dh).transpose(1, 0, 3, 4, 2, 5)
    sc = sel.reshape(B, G, n_chunk, SEL_QCHUNK, n_top).transpose(2, 0, 1, 3, 4)
    b_ix = jnp.arange(B)[:, None, None, None]
    g_ix = jnp.arange(G)[None, :, None, None]
    nk = n_top * SEL_LEN

    def chunk(args):
        q_blk, s_blk, ci = args
        k_g = kb[b_ix, g_ix, s_blk]
        v_g = vb[b_ix, g_ix, s_blk]
        s = jnp.einsum('bghqd,bgqnld->bghqnl', q_blk, k_g) * scale
        kpos = s_blk[..., None] * SEL_LEN + jnp.arange(SEL_LEN)
        qpos = ci * SEL_QCHUNK + jnp.arange(SEL_QCHUNK)
        mask = kpos <= qpos[:, None, None]
        p = masked_softmax(s.reshape(B, G, QPG, SEL_QCHUNK, nk),
                           mask.reshape(B, G, 1, SEL_QCHUNK, nk))
        return jnp.einsum('bghqk,bgqkd->bghqd', p.astype(v.dtype),
                          v_g.reshape(B, G, SEL_QCHUNK, nk, dh))

    o = lax.map(chunk, (qc, sc, jnp.arange(n_chunk)))
    return o.transpose(1, 0, 4, 2, 3, 5).reshape(B, S, G, QPG, dh)


def compress_blocks(blk, pe, w1, w2):
    z = jnp.einsum('bnlgd,lde->bnge', blk + pe[:, None, :], w1)
    return jax.nn.gelu(z) @ w2


def nsa_mixer(h, positions, cos, sin, w_in, cmp_pe, cmp_w1, cmp_w2, w_out):
    B, S, _ = h.shape
    HD = A_HEADS * HEAD_DIM
    KD = A_GROUPS * HEAD_DIM
    proj = h @ w_in
    q = proj[..., :HD].reshape(B, S, A_GROUPS, A_QPG, HEAD_DIM)
    kv = proj[..., HD:HD + 6 * KD].reshape(B, S, 6, A_GROUPS, HEAD_DIM)
    gate = jax.nn.sigmoid(proj[..., HD + 6 * KD:].reshape(B, S, A_GROUPS, A_QPG, 3))
    k_c_raw, v_c_raw = kv[:, :, 0], kv[:, :, 1]
    k_s, v_s = kv[:, :, 2], kv[:, :, 3]
    k_w, v_w = kv[:, :, 4], kv[:, :, 5]
    q = apply_rope(q, cos, sin)
    k_s = apply_rope(k_s, cos, sin)
    k_w = apply_rope(k_w, cos, sin)
    scale = HEAD_DIM ** -0.5
    t = np.arange(S)
    n_cmp = (S - CMP_LEN) // CMP_STRIDE + 1
    blk_idx = np.arange(n_cmp)[:, None] * CMP_STRIDE + np.arange(CMP_LEN)[None, :]
    k_c = compress_blocks(k_c_raw[:, blk_idx], cmp_pe[0], cmp_w1[0], cmp_w2[0])
    v_c = compress_blocks(v_c_raw[:, blk_idx], cmp_pe[1], cmp_w1[1], cmp_w2[1])
    end_pos = blk_idx[:, -1]
    ccos, csin = rope_cos_sin(positions[:, end_pos])
    k_c = apply_rope(k_c, ccos, csin)
    s_c = jnp.einsum('bsghd,bngd->bghsn', q, k_c) * scale
    p_c = masked_softmax(s_c, end_pos[None, :] <= t[:, None])
    o_c = jnp.einsum('bghsn,bngd->bsghd', p_c.astype(v_c.dtype), v_c)
    n_sel = S // SEL_LEN
    n_top = min(SEL_TOPN, n_sel)
    cmp_lo = np.arange(n_cmp) * CMP_STRIDE
    sel_lo = np.arange(n_sel) * SEL_LEN
    cover = ((cmp_lo[:, None] < sel_lo[None, :] + SEL_LEN)
             & (sel_lo[None, :] < cmp_lo[:, None] + CMP_LEN)).astype(np.float32)
    imp = jnp.einsum('bghsn,nj->bgsj', p_c, jnp.asarray(cover))
    cur = t // SEL_LEN
    j = np.arange(n_sel)
    forced = (j[None] == 0) | (j[None] == cur[:, None]) | (j[None] == cur[:, None] - 1)
    causal = j[None] <= cur[:, None]
    imp = jnp.where(forced, jnp.inf, jnp.where(causal, imp, -jnp.inf))
    _, sel = lax.top_k(imp, n_top)
    o_s = selected_block_attention(q, k_s, v_s, sel)
    o_w = banded_attention(q, k_w, v_w, A_WINDOW)
    o = gate[..., 0:1] * o_c + gate[..., 1:2] * o_s + gate[..., 2:3] * o_w
    return o.reshape(B, S, HD) @ w_out


def shared_kv(h, c_act, cos, sin, kv_norm_g, kv_mod_w, kv_mod_b, kv_w, kv_b):
    B, S, _ = h.shape
    shift, scale = ada_mod(c_act, kv_mod_w, kv_mod_b, 2)
    hn = rms_norm(h, kv_norm_g) * (1 + scale) + shift
    kv = (hn @ kv_w + kv_b).reshape(B, S, 2, B_KV_HEADS, HEAD_DIM)
    return apply_rope(kv[:, :, 0], cos, sin), kv[:, :, 1]


def sink_swa_mixer(h, cos, sin, k_sh, v_sh, w_q, b_q, sinks, w_out, b_out):
    B, S, _ = h.shape
    q = (h @ w_q + b_q).reshape(B, S, B_KV_HEADS, B_QPG, HEAD_DIM)
    q = apply_rope(q, cos, sin)
    o = banded_attention(q, k_sh, v_sh, B_WINDOW, sinks)
    return o.reshape(B, S, B_HEADS * HEAD_DIM) @ w_out + b_out


def moe_ffn(h, router_w, router_b, w_gate_up, b_gate_up, w_down, b_down):
    B, S, D = h.shape
    T = B * S
    TK = T * TOP_K
    xt = h.reshape(T, D)
    logits = (xt @ router_w + router_b).astype(jnp.float32)
    top_v, top_i = lax.top_k(logits, TOP_K)
    wts = jax.nn.softmax(top_v, axis=-1).astype(h.dtype)
    e = top_i.reshape(TK)
    order = jnp.argsort(e)
    e_sorted = e[order]
    counts = jnp.bincount(e, length=N_EXPERTS)
    start = jnp.cumsum(counts) - counts
    pad_counts = (counts + MOE_BLOCK - 1) // MOE_BLOCK * MOE_BLOCK
    pad_end = jnp.cumsum(pad_counts)
    pad_start = pad_end - pad_counts
    slot_sorted = (pad_start[e_sorted] + jnp.arange(TK) - start[e_sorted]).astype(jnp.int32)
    n_slots = TK + N_EXPERTS * MOE_BLOCK
    n_blk = n_slots // MOE_BLOCK
    tok_of_slot = jnp.zeros((n_slots,), jnp.int32).at[slot_sorted].set((order // TOP_K).astype(jnp.int32))
    slot = jnp.zeros((TK,), jnp.int32).at[order].set(slot_sorted).reshape(T, TOP_K)
    blk_start = jnp.arange(n_blk) * MOE_BLOCK
    blk_expert = jnp.minimum(jnp.sum(pad_end[None, :] <= blk_start[:, None], axis=1), N_EXPERTS - 1)

    def expert_block(args):
        tok, ex = args
        xb = xt[tok]
        gu = xb @ w_gate_up[ex] + b_gate_up[ex]
        glu = jnp.minimum(gu[:, :D_FF], SWIGLU_LIMIT)
        lin = jnp.clip(gu[:, D_FF:], -SWIGLU_LIMIT, SWIGLU_LIMIT)
        act = glu * jax.nn.sigmoid(SWIGLU_ALPHA * glu) * (lin + 1.0)
        return act @ w_down[ex] + b_down[ex]

    out = lax.map(expert_block, (tok_of_slot.reshape(n_blk, MOE_BLOCK), blk_expert))
    out = out.reshape(n_slots, D)
    y = jnp.zeros_like(xt)
    for kk in range(TOP_K):
        y = y + wts[:, kk:kk + 1] * out[slot[:, kk]]
    return y.reshape(B, S, D)


def setup_inputs(seed: int = 0) -> dict:
    key = jax.random.key(seed)
    ks = jax.random.split(key, 32)
    f32 = jnp.float32
    D = D_MODEL
    HD = HEAD_DIM

    def nrm(k, shape, s):
        return jax.random.normal(k, shape, f32) * s

    x = nrm(ks[0], (BATCH, SEQ, D), 1.0)
    c = nrm(ks[1], (BATCH, D), 1.0)
    offset = jax.random.randint(ks[2], (BATCH, 1), 0, 1024, dtype=jnp.int32)
    positions = offset + jnp.arange(SEQ, dtype=jnp.int32)[None, :]
    mod_w = nrm(ks[3], (DEPTH, 2, D, 3 * D), 0.5 * D ** -0.5)
    mod_b = nrm(ks[4], (DEPTH, 2, 3 * D), 0.02)
    norm_g = 1.0 + nrm(ks[5], (DEPTH, 2, D), 0.02)
    final_norm_g = 1.0 + nrm(ks[6], (D,), 0.02)
    a_w_in = nrm(ks[7], (N_A_LAYERS, D, A_PROJ), D ** -0.5)
    a_cmp_pe = nrm(ks[8], (N_A_LAYERS, 2, CMP_LEN, HD), 0.1)
    a_cmp_w1 = nrm(ks[9], (N_A_LAYERS, 2, CMP_LEN, HD, CMP_HID), (CMP_LEN * HD) ** -0.5)
    a_cmp_w2 = nrm(ks[10], (N_A_LAYERS, 2, CMP_HID, HD), CMP_HID ** -0.5)
    a_w_out = nrm(ks[11], (N_A_LAYERS, A_HEADS * HD, D), (A_HEADS * HD) ** -0.5)
    kv_norm_g = 1.0 + nrm(ks[12], (D,), 0.02)
    kv_mod_w = nrm(ks[13], (D, 2 * D), 0.5 * D ** -0.5)
    kv_mod_b = nrm(ks[14], (2 * D,), 0.02)
    kv_w = nrm(ks[15], (D, 2 * B_KV_HEADS * HD), D ** -0.5)
    kv_b = nrm(ks[16], (2 * B_KV_HEADS * HD,), 0.02)
    b_w_q = nrm(ks[17], (N_B_LAYERS, D, B_HEADS * HD), D ** -0.5)
    b_b_q = nrm(ks[18], (N_B_LAYERS, B_HEADS * HD), 0.02)
    b_sinks = nrm(ks[19], (N_B_LAYERS, B_HEADS), 1.0)
    b_w_out = nrm(ks[20], (N_B_LAYERS, B_HEADS * HD, D), (B_HEADS * HD) ** -0.5)
    b_b_out = nrm(ks[21], (N_B_LAYERS, D), 0.02)
    router_w = nrm(ks[22], (DEPTH, D, N_EXPERTS), D ** -0.5)
    router_b = nrm(ks[23], (DEPTH, N_EXPERTS), 0.01)
    w_gate_up = nrm(ks[24], (DEPTH, N_EXPERTS, D, 2 * D_FF), D ** -0.5)
    b_gate_up = nrm(ks[25], (DEPTH, N_EXPERTS, 2 * D_FF), 0.01)
    w_down = nrm(ks[26], (DEPTH, N_EXPERTS, D_FF, D), D_FF ** -0.5)
    b_down = nrm(ks[27], (DEPTH, N_EXPERTS, D), 0.01)
    return {'x': x, 'c': c, 'positions': positions, 'mod_w': mod_w, 'mod_b': mod_b,
            'norm_g': norm_g, 'final_norm_g': final_norm_g, 'a_w_in': a_w_in,
            'a_cmp_pe': a_cmp_pe, 'a_cmp_w1': a_cmp_w1, 'a_cmp_w2': a_cmp_w2,
            'a_w_out': a_w_out, 'kv_norm_g': kv_norm_g, 'kv_mod_w': kv_mod_w,
            'kv_mod_b': kv_mod_b, 'kv_w': kv_w, 'kv_b': kv_b, 'b_w_q': b_w_q,
            'b_b_q': b_b_q, 'b_sinks': b_sinks, 'b_w_out': b_w_out, 'b_b_out': b_b_out,
            'router_w': router_w, 'router_b': router_b, 'w_gate_up': w_gate_up,
            'b_gate_up': b_gate_up, 'w_down': w_down, 'b_down': b_down}


def reference(x, c, positions, mod_w, mod_b, norm_g, final_norm_g, a_w_in, a_cmp_pe,
              a_cmp_w1, a_cmp_w2, a_w_out, kv_norm_g, kv_mod_w, kv_mod_b, kv_w, kv_b,
              b_w_q, b_b_q, b_sinks, b_w_out, b_b_out, router_w, router_b, w_gate_up,
              b_gate_up, w_down, b_down):
    c_act = jax.nn.silu(c)
    cos, sin = rope_cos_sin(positions)
    h = x
    k_sh = None
    v_sh = None
    for layer in range(DEPTH):
        if layer == N_A_LAYERS:
            k_sh, v_sh = shared_kv(h, c_act, cos, sin, kv_norm_g, kv_mod_w, kv_mod_b, kv_w, kv_b)
        shift, scale, gate = ada_mod(c_act, mod_w[layer, 0], mod_b[layer, 0], 3)
        hn = rms_norm(h, norm_g[layer, 0]) * (1 + scale) + shift
        if layer < N_A_LAYERS:
            i = layer
            mix = nsa_mixer(hn, positions, cos, sin, a_w_in[i], a_cmp_pe[i], a_cmp_w1[i],
                            a_cmp_w2[i], a_w_out[i])
        else:
            i = layer - N_A_LAYERS
            mix = sink_swa_mixer(hn, cos, sin, k_sh, v_sh, b_w_q[i], b_b_q[i], b_sinks[i],
                                 b_w_out[i], b_b_out[i])
        h = h + gate * mix
        shift, scale, gate = ada_mod(c_act, mod_w[layer, 1], mod_b[layer, 1], 3)
        hn = rms_norm(h, norm_g[layer, 1]) * (1 + scale) + shift
        h = h + gate * moe_ffn(hn, router_w[layer], router_b[layer], w_gate_up[layer],
                               b_gate_up[layer], w_down[layer], b_down[layer])
    return rms_norm(h, final_norm_g)
```

```python
import functools

import numpy as np
import jax
import jax.numpy as jnp
from jax import lax
from jax.experimental import pallas as pl
from jax.experimental.pallas import tpu as pltpu

F32 = jnp.float32
BF16 = jnp.bfloat16
I32 = jnp.int32

D_MODEL = 1024
HEAD_DIM = 64
ROT_HALF = 8
ROPE_THETA = 500000.0
NORM_EPS = 1e-5
QK_SCALE = HEAD_DIM ** -0.5
A_GROUPS = 4
A_QPG = 4
CMP_LEN = 32
CMP_STRIDE = 16
CMP_HID = 128
SEL_LEN = 64
SEL_TOPN = 16
A_WINDOW = 512
B_KV_HEADS = 2
B_QPG = 8
B_WINDOW = 128
N_EXPERTS = 32
TOP_K = 4
D_FF = 1024
SWIGLU_LIMIT = 7.0
SWIGLU_ALPHA = 1.702
MOE_BLOCK = 256

LANES = 128
TQ = 128
KC = 128
TM = 512
GATHER_ROWS = 512
MASKED = -1e30
M_INIT = -1e29
VMEM_LIMIT = 48 * 1024 * 1024


def _cparams(*sem):
    return pltpu.CompilerParams(dimension_semantics=sem, vmem_limit_bytes=VMEM_LIMIT)


def _dot(a, b):
    return jnp.dot(a, b, preferred_element_type=F32)


def _norm_mod(x, g, scale, shift):
    y = x * lax.rsqrt(jnp.mean(x * x, axis=-1, keepdims=True) + NORM_EPS)
    return (y * g) * (1.0 + scale) + shift


def _rope128(v, rc, ra, rb):
    return v * rc + pltpu.roll(v, LANES - ROT_HALF, 1) * ra + pltpu.roll(v, ROT_HALF, 1) * rb


def _rope(v, rc, ra, rb):
    parts = [_rope128(v[:, s * LANES:(s + 1) * LANES], rc, ra, rb) for s in range(v.shape[1] // LANES)]
    return parts[0] if len(parts) == 1 else jnp.concatenate(parts, axis=1)


def _mods_kernel(c_ref, w_ref, b_ref, o_ref):
    c = c_ref[...]
    ca = c * jax.nn.sigmoid(c)
    o_ref[0] = jnp.dot(ca, w_ref[0], preferred_element_type=F32,
                       precision=lax.Precision.HIGHEST) + b_ref[0]


def _mods(c, w, b):
    n, _, N = w.shape
    bsz = c.shape[0]
    tn = 1024
    return pl.pallas_call(
        _mods_kernel,
        grid=(n, N // tn),
        in_specs=[pl.BlockSpec((bsz, D_MODEL), lambda i, j: (0, 0)),
                  pl.BlockSpec((1, D_MODEL, tn), lambda i, j: (i, 0, j)),
                  pl.BlockSpec((1, 1, tn), lambda i, j: (i, 0, j))],
        out_specs=pl.BlockSpec((1, bsz, tn), lambda i, j: (i, 0, j)),
        out_shape=jax.ShapeDtypeStruct((n, bsz, N), F32),
        compiler_params=_cparams("parallel", "parallel"),
        name="adaln_mods",
    )(c, w, b.reshape(n, 1, N))


def _nsa_proj_kernel(x_ref, g_ref, sc_ref, sh_ref, wq_ref, wkv_ref, wg_ref, rc_ref, ra_ref, rb_ref,
                     q_ref, kv_ref, gate_ref):
    hn = _norm_mod(x_ref[...], g_ref[...], sc_ref[0], sh_ref[0]).astype(BF16)
    rc, ra, rb = rc_ref[...], ra_ref[...], rb_ref[...]
    q = _dot(hn, wq_ref[...])
    q_ref[...] = (_rope(q, rc, ra, rb) * QK_SCALE).astype(BF16)
    kv = _dot(hn, wkv_ref[...])
    kd = A_GROUPS * HEAD_DIM
    for j in range(6):
        piece = kv[:, j * kd:(j + 1) * kd]
        if j in (2, 4):
            piece = _rope(piece, rc, ra, rb)
        kv_ref[:, j * kd:(j + 1) * kd] = piece.astype(BF16)
    gate_ref[...] = jax.nn.sigmoid(_dot(hn, wg_ref[...]))


def _nsa_proj(x, g, scale, shift, wq, wkv, wg, rc, ra, rb, seq):
    T = x.shape[0]
    per_b = seq // TM
    row = lambda i: (i, 0)
    fixed = lambda i: (0, 0)
    bat = lambda i: (i // per_b, 0, 0)
    return pl.pallas_call(
        _nsa_proj_kernel,
        grid=(T // TM,),
        in_specs=[pl.BlockSpec((TM, D_MODEL), row),
                  pl.BlockSpec((1, D_MODEL), fixed),
                  pl.BlockSpec((1, 1, D_MODEL), bat),
                  pl.BlockSpec((1, 1, D_MODEL), bat),
                  pl.BlockSpec(wq.shape, fixed),
                  pl.BlockSpec(wkv.shape, fixed),
                  pl.BlockSpec(wg.shape, fixed),
                  pl.BlockSpec((TM, LANES), row),
                  pl.BlockSpec((TM, LANES), row),
                  pl.BlockSpec((TM, LANES), row)],
        out_specs=[pl.BlockSpec((TM, wq.shape[1]), row),
                   pl.BlockSpec((TM, wkv.shape[1]), row),
                   pl.BlockSpec((TM, LANES), row)],
        out_shape=[jax.ShapeDtypeStruct((T, wq.shape[1]), BF16),
                   jax.ShapeDtypeStruct((T, wkv.shape[1]), BF16),
                   jax.ShapeDtypeStruct((T, LANES), F32)],
        compiler_params=_cparams("parallel"),
        name="nsa_proj",
    )(x, g, scale, shift, wq, wkv, wg, rc, ra, rb)


def _compress_kernel(c_ref, pe_ref, w1_ref, w2_ref, rc_ref, ra_ref, rb_ref, o_ref):
    c = c_ref[0, 0, 0].astype(F32)
    a = _dot((c + pe_ref[0, 0:1, :]).astype(BF16), w1_ref[0, 0])
    b = _dot((c + pe_ref[0, 1:2, :]).astype(BF16), w1_ref[0, 1])
    z = a + pltpu.roll(b, b.shape[0] - 1, 0)
    hid = jax.nn.gelu(z)
    out = _dot(hid.astype(BF16), w2_ref[0])
    out = _rope128(out, rc_ref[0, 0], ra_ref[0, 0], rb_ref[0, 0])
    o_ref[0, 0, 0] = out[:, :HEAD_DIM].astype(BF16)


def _compress(craw, pe, w1, w2, rc, ra, rb):
    _, bsz, G, nch, width = craw.shape
    return pl.pallas_call(
        _compress_kernel,
        grid=(2, bsz, G),
        in_specs=[pl.BlockSpec((1, 1, 1, nch, width), lambda s, b, g: (s, b, g, 0, 0)),
                  pl.BlockSpec((1, 2, width), lambda s, b, g: (s, 0, 0)),
                  pl.BlockSpec((1, 2, width, CMP_HID), lambda s, b, g: (s, 0, 0, 0)),
                  pl.BlockSpec((1, CMP_HID, LANES), lambda s, b, g: (s, 0, 0)),
                  pl.BlockSpec((1, 1, nch, LANES), lambda s, b, g: (s, b, 0, 0)),
                  pl.BlockSpec((1, 1, nch, LANES), lambda s, b, g: (s, b, 0, 0)),
                  pl.BlockSpec((1, 1, nch, LANES), lambda s, b, g: (s, b, 0, 0))],
        out_specs=pl.BlockSpec((1, 1, 1, nch, HEAD_DIM), lambda s, b, g: (s, b, g, 0, 0)),
        out_shape=jax.ShapeDtypeStruct((2, bsz, G, nch, HEAD_DIM), BF16),
        compiler_params=_cparams("parallel", "parallel", "parallel"),
        name="nsa_compress",
    )(craw, pe, w1, w2, rc, ra, rb)


def _flash_step(k, vT, qT, bias, m, l, acc):
    s = _dot(k, qT) + bias
    m_new = jnp.maximum(m, jnp.max(s, axis=0, keepdims=True))
    alpha = jnp.exp(m - m_new)
    p = jnp.exp(s - m_new)
    l_new = alpha * l + jnp.sum(p, axis=0, keepdims=True)
    acc_new = alpha * acc + _dot(vT, p.astype(BF16))
    return m_new, l_new, acc_new


def _tile_lanes(x, n):
    return x if n == 1 else jnp.concatenate([x] * n, axis=1)


def _heads_to_rows(oT, n_heads):
    stacked = jnp.concatenate([oT[:, h * TQ:(h + 1) * TQ] for h in range(n_heads)], axis=0)
    return stacked.T


def _nsa_attn_kernel(qT_ref, kc_ref, vcT_ref, ks_ref, vsT_ref, kw_ref, vwT_ref, gT_ref, covT_ref,
                     o_ref, selb_ref):
    qi = pl.program_id(2)
    H = A_QPG
    W = H * TQ
    qT = qT_ref[0, 0, 0]
    qpos = qi * TQ + lax.broadcasted_iota(I32, (1, TQ), 1)
    ncmp = kc_ref.shape[2]

    n_io = lax.broadcasted_iota(I32, (ncmp, 1), 0)
    end_pos = n_io * CMP_STRIDE + (CMP_LEN - 1)
    valid = (end_pos <= qpos) & (n_io < ncmp - 1)
    bias_c = _tile_lanes(jnp.where(valid, 0.0, MASKED), H)
    s = _dot(kc_ref[0, 0], qT) + bias_c
    m = jnp.maximum(jnp.max(s, axis=0, keepdims=True), M_INIT)
    p = jnp.exp(s - m)
    den = jnp.maximum(jnp.sum(p, axis=0, keepdims=True), jnp.finfo(F32).tiny)
    pn = p / den
    o_c = _dot(vcT_ref[0, 0], pn.astype(BF16))

    psum = pn[:, 0:TQ]
    for h in range(1, H):
        psum = psum + pn[:, h * TQ:(h + 1) * TQ]
    p_hi = psum.astype(BF16)
    p_lo = (psum - p_hi.astype(F32)).astype(BF16)
    covT = covT_ref[...]
    impT = _dot(covT, p_hi) + _dot(covT, p_lo)

    n_sel = covT.shape[0]
    j_io = lax.broadcasted_iota(I32, (n_sel, 1), 0)
    cur = qpos // SEL_LEN
    forced = (j_io == 0) | (j_io == cur) | (j_io == cur - 1)
    causal = j_io <= cur
    score = jnp.where(forced, jnp.inf, jnp.where(causal, impT, -jnp.inf))
    rank = jnp.zeros((n_sel, TQ), I32)
    for i in range(n_sel):
        vi = score[i:i + 1, :]
        ahead = (vi > score) | ((vi == score) & (i < j_io))
        rank = rank + ahead.astype(I32)
    selb = jnp.where(rank < SEL_TOPN, 0.0, MASKED)
    for j in range(n_sel):
        selb_ref[j] = selb[j:j + 1, :]

    k_io = lax.broadcasted_iota(I32, (KC, 1), 0)
    half = KC // SEL_LEN
    init = (jnp.full((1, W), M_INIT, F32), jnp.zeros((1, W), F32), jnp.zeros((HEAD_DIM, W), F32))

    def sel_body(c, carry):
        kpos = c * KC + k_io
        blk = jnp.concatenate(
            [jnp.broadcast_to(selb_ref[half * c + r], (SEL_LEN, TQ)) for r in range(half)], axis=0)
        bias = _tile_lanes(jnp.where(kpos <= qpos, blk, MASKED), H)
        return _flash_step(ks_ref[0, 0, c], vsT_ref[0, 0, c], qT, bias, *carry)

    _, l_s, acc_s = lax.fori_loop(0, qi + 1, sel_body, init)

    def win_body(c, carry):
        kpos = c * KC + k_io
        ok = (kpos <= qpos) & (kpos > qpos - A_WINDOW)
        bias = _tile_lanes(jnp.where(ok, 0.0, MASKED), H)
        return _flash_step(kw_ref[0, 0, c], vwT_ref[0, 0, c], qT, bias, *carry)

    _, l_w, acc_w = lax.fori_loop(jnp.maximum(qi - A_WINDOW // KC, 0), qi + 1, win_body, init)

    g = gT_ref[0, 0, 0]
    oT = g[0:1, :] * o_c + g[1:2, :] * (acc_s / l_s) + g[2:3, :] * (acc_w / l_w)
    o_ref[0] = _heads_to_rows(oT, H).astype(BF16)


def _nsa_attn(qT, kc, vcT, ks, vsT, kw, vwT, gT, covT, seq):
    bsz, G, nq = qT.shape[:3]
    W = A_QPG * TQ
    nch = ks.shape[2]
    ncmp = kc.shape[2]
    n_sel = covT.shape[0]
    per_bg = lambda b, g, i: (b, g, 0, 0)
    per_bg5 = lambda b, g, i: (b, g, 0, 0, 0)
    return pl.pallas_call(
        _nsa_attn_kernel,
        grid=(bsz, G, nq),
        in_specs=[pl.BlockSpec((1, 1, 1, HEAD_DIM, W), lambda b, g, i: (b, g, i, 0, 0)),
                  pl.BlockSpec((1, 1, ncmp, HEAD_DIM), per_bg),
                  pl.BlockSpec((1, 1, HEAD_DIM, ncmp), per_bg),
                  pl.BlockSpec((1, 1, nch, KC, HEAD_DIM), per_bg5),
                  pl.BlockSpec((1, 1, nch, HEAD_DIM, KC), per_bg5),
                  pl.BlockSpec((1, 1, nch, KC, HEAD_DIM), per_bg5),
                  pl.BlockSpec((1, 1, nch, HEAD_DIM, KC), per_bg5),
                  pl.BlockSpec((1, 1, 1, 8, W), lambda b, g, i: (b, g, i, 0, 0)),
                  pl.BlockSpec((n_sel, ncmp), lambda b, g, i: (0, 0))],
        out_specs=pl.BlockSpec((1, TQ, A_QPG * HEAD_DIM), lambda b, g, i: (b, i, g)),
        out_shape=jax.ShapeDtypeStruct((bsz, seq, D_MODEL), BF16),
        scratch_shapes=[pltpu.VMEM((n_sel, 1, TQ), F32)],
        compiler_params=_cparams("parallel", "parallel", "arbitrary"),
        name="nsa_attention",
    )(qT, kc, vcT, ks, vsT, kw, vwT, gT, covT)


def _swa_attn_kernel(qT_ref, k_ref, vT_ref, sink_ref, o_ref):
    qi = pl.program_id(2)
    H = B_QPG
    W = H * TQ
    half_w = W // 2
    qpos = qi * TQ + lax.broadcasted_iota(I32, (1, TQ), 1)
    k_io = lax.broadcasted_iota(I32, (KC, 1), 0)
    chunks = (jnp.maximum(qi - 1, 0), qi)
    biases = []
    for idx, c in enumerate(chunks):
        kpos = c * KC + k_io
        ok = (kpos <= qpos) & (kpos > qpos - B_WINDOW)
        if idx == 0:
            ok = ok & (qi > 0)
        biases.append(jnp.where(ok, 0.0, MASKED))
    outs = []
    for part in range(2):
        lanes = slice(part * half_w, (part + 1) * half_w)
        qT = qT_ref[0, 0, 0][:, lanes]
        m = sink_ref[0][:, lanes]
        l = jnp.ones((1, half_w), F32)
        acc = jnp.zeros((HEAD_DIM, half_w), F32)
        for idx, c in enumerate(chunks):
            bias = _tile_lanes(biases[idx], H // 2)
            m, l, acc = _flash_step(k_ref[0, 0, c], vT_ref[0, 0, c], qT, bias, m, l, acc)
        outs.append(acc / l)
    oT = jnp.concatenate(outs, axis=1)
    o_ref[0] = _heads_to_rows(oT, H).astype(BF16)


def _swa_attn(qT, k, vT, sink_rows, seq):
    bsz, G, nq = qT.shape[:3]
    W = B_QPG * TQ
    nch = k.shape[2]
    per_bg5 = lambda b, g, i: (b, g, 0, 0, 0)
    return pl.pallas_call(
        _swa_attn_kernel,
        grid=(bsz, G, nq),
        in_specs=[pl.BlockSpec((1, 1, 1, HEAD_DIM, W), lambda b, g, i: (b, g, i, 0, 0)),
                  pl.BlockSpec((1, 1, nch, KC, HEAD_DIM), per_bg5),
                  pl.BlockSpec((1, 1, nch, HEAD_DIM, KC), per_bg5),
                  pl.BlockSpec((1, 1, W), lambda b, g, i: (g, 0, 0))],
        out_specs=pl.BlockSpec((1, TQ, B_QPG * HEAD_DIM), lambda b, g, i: (b, i, g)),
        out_shape=jax.ShapeDtypeStruct((bsz, seq, D_MODEL), BF16),
        compiler_params=_cparams("parallel", "parallel", "arbitrary"),
        name="swa_attention",
    )(qT, k, vT, sink_rows)


def _swa_proj_kernel(x_ref, gkv_ref, sckv_ref, shkv_ref, gq_ref, scq_ref, shq_ref,
                     wkv_ref, bkv_ref, wq_ref, bq_ref, rc_ref, ra_ref, rb_ref, k_ref, v_ref, q_ref):
    x = x_ref[...]
    rc, ra, rb = rc_ref[...], ra_ref[...], rb_ref[...]
    hkv = _norm_mod(x, gkv_ref[...], sckv_ref[0], shkv_ref[0]).astype(BF16)
    kv = _dot(hkv, wkv_ref[...]) + bkv_ref[...]
    kw = B_KV_HEADS * HEAD_DIM
    k_ref[...] = _rope(kv[:, :kw], rc, ra, rb).astype(BF16)
    v_ref[...] = kv[:, kw:].astype(BF16)
    hq = _norm_mod(x, gq_ref[...], scq_ref[0], shq_ref[0]).astype(BF16)
    q = _dot(hq, wq_ref[...]) + bq_ref[...]
    q_ref[...] = (_rope(q, rc, ra, rb) * QK_SCALE).astype(BF16)


def _swa_proj(x, gkv, sckv, shkv, gq, scq, shq, wkv, bkv, wq, bq, rc, ra, rb, seq):
    T = x.shape[0]
    per_b = seq // TM
    row = lambda i: (i, 0)
    fixed = lambda i: (0, 0)
    bat = lambda i: (i // per_b, 0, 0)
    kw = B_KV_HEADS * HEAD_DIM
    vec = pl.BlockSpec((1, D_MODEL), fixed)
    mod = pl.BlockSpec((1, 1, D_MODEL), bat)
    tab = pl.BlockSpec((TM, LANES), row)
    return pl.pallas_call(
        _swa_proj_kernel,
        grid=(T // TM,),
        in_specs=[pl.BlockSpec((TM, D_MODEL), row), vec, mod, mod, vec, mod, mod,
                  pl.BlockSpec(wkv.shape, fixed), pl.BlockSpec(bkv.shape, fixed),
                  pl.BlockSpec(wq.shape, fixed), pl.BlockSpec(bq.shape, fixed), tab, tab, tab],
        out_specs=[pl.BlockSpec((TM, kw), row), pl.BlockSpec((TM, kw), row),
                   pl.BlockSpec((TM, D_MODEL), row)],
        out_shape=[jax.ShapeDtypeStruct((T, kw), BF16), jax.ShapeDtypeStruct((T, kw), BF16),
                   jax.ShapeDtypeStruct((T, D_MODEL), BF16)],
        compiler_params=_cparams("parallel"),
        name="swa_proj",
    )(x, gkv, sckv, shkv, gq, scq, shq, wkv, bkv, wq, bq, rc, ra, rb)


def _oproj_kernel(x_ref, w_ref, b_ref, res_ref, gate_ref, o_ref):
    y = _dot(x_ref[...], w_ref[...]) + b_ref[...]
    o_ref[...] = res_ref[...] + gate_ref[0] * y


def _oproj(x, w, b, res, gate, seq):
    T = x.shape[0]
    per_b = seq // TM
    row = lambda i: (i, 0)
    fixed = lambda i: (0, 0)
    return pl.pallas_call(
        _oproj_kernel,
        grid=(T // TM,),
        in_specs=[pl.BlockSpec((TM, x.shape[1]), row), pl.BlockSpec(w.shape, fixed),
                  pl.BlockSpec((1, D_MODEL), fixed), pl.BlockSpec((TM, D_MODEL), row),
                  pl.BlockSpec((1, 1, D_MODEL), lambda i: (i // per_b, 0, 0))],
        out_specs=pl.BlockSpec((TM, D_MODEL), row),
        out_shape=jax.ShapeDtypeStruct((T, D_MODEL), F32),
        compiler_params=_cparams("parallel"),
        name="out_proj",
    )(x, w, b, res, gate)


def _router_kernel(x_ref, g_ref, sc_ref, sh_ref, wh_ref, wl_ref, b_ref, hn_ref, tw_ref):
    hn = _norm_mod(x_ref[...], g_ref[...], sc_ref[0], sh_ref[0])
    hn_ref[...] = hn
    hi = hn.astype(BF16)
    lo = (hn - hi.astype(F32)).astype(BF16)
    wh = wh_ref[...]
    logits = _dot(hi, wh) + _dot(lo, wh) + _dot(hi, wl_ref[...]) + b_ref[...]
    col = lax.broadcasted_iota(I32, logits.shape, 1)
    cur = jnp.where(col < N_EXPERTS, logits, -jnp.inf)
    vals, idxs = [], []
    for _ in range(TOP_K):
        mx = jnp.max(cur, axis=-1, keepdims=True)
        ix = jnp.min(jnp.where(cur == mx, col, LANES), axis=-1, keepdims=True)
        vals.append(mx)
        idxs.append(ix)
        cur = jnp.where(col == ix, -jnp.inf, cur)
    es = [jnp.exp(v - vals[0]) for v in vals]
    den = es[0] + es[1] + es[2] + es[3]
    packed = jnp.zeros(logits.shape, F32)
    for k in range(TOP_K):
        packed = jnp.where(col == k, idxs[k].astype(F32), packed)
        packed = jnp.where(col == TOP_K + k, es[k] / den, packed)
    tw_ref[...] = packed.T[:2 * TOP_K, :]


def _router(x, g, scale, shift, wh, wl, b, seq):
    T = x.shape[0]
    per_b = seq // TM
    row = lambda i: (i, 0)
    fixed = lambda i: (0, 0)
    bat = lambda i: (i // per_b, 0, 0)
    return pl.pallas_call(
        _router_kernel,
        grid=(T // TM,),
        in_specs=[pl.BlockSpec((TM, D_MODEL), row), pl.BlockSpec((1, D_MODEL), fixed),
                  pl.BlockSpec((1, 1, D_MODEL), bat), pl.BlockSpec((1, 1, D_MODEL), bat),
                  pl.BlockSpec(wh.shape, fixed), pl.BlockSpec(wl.shape, fixed),
                  pl.BlockSpec((1, LANES), fixed)],
        out_specs=[pl.BlockSpec((TM, D_MODEL), row), pl.BlockSpec((2 * TOP_K, TM), lambda i: (0, i))],
        out_shape=[jax.ShapeDtypeStruct((T, D_MODEL), F32),
                   jax.ShapeDtypeStruct((2 * TOP_K, T), F32)],
        compiler_params=_cparams("parallel"),
        name="moe_router",
    )(x, g, scale, shift, wh, wl, b)


def _gather_kernel(idx_ref, src_ref, dst_ref, sem):
    i = pl.program_id(0)
    n = pl.num_programs(0)
    R = idx_ref.shape[-1]

    def row_copy(src_row, dst_row, slot):
        return pltpu.make_async_copy(src_ref.at[pl.ds(src_row, 1)], dst_ref.at[pl.ds(dst_row, 1)], sem.at[slot])

    def issue(r, carry):
        row_copy(idx_ref[0, 0, r], i * R + r, i % 2).start()
        return carry

    lax.fori_loop(0, R, issue, 0)

    def drain(step, slot):
        def body(r, carry):
            row_copy(0, step * R + r, slot).wait()
            return carry
        lax.fori_loop(0, R, body, 0)

    @pl.when(i > 0)
    def _():
        drain(i - 1, (i + 1) % 2)

    @pl.when(i == n - 1)
    def _():
        drain(i, i % 2)


def _gather_rows(src, idx):
    M = idx.shape[0]
    R = GATHER_ROWS
    return pl.pallas_call(
        _gather_kernel,
        grid=(M // R,),
        in_specs=[pl.BlockSpec((1, 1, R), lambda i: (i, 0, 0), memory_space=pltpu.SMEM),
                  pl.BlockSpec(memory_space=pl.ANY)],
        out_specs=pl.BlockSpec(memory_space=pl.ANY),
        out_shape=jax.ShapeDtypeStruct((M, src.shape[1]), src.dtype),
        scratch_shapes=[pltpu.SemaphoreType.DMA((2,))],
        compiler_params=_cparams("arbitrary"),
        name="row_gather",
    )(idx.reshape(M // R, 1, R), src)


def _experts_kernel(be_ref, nu_ref, x_ref, wgu_ref, bgu_ref, wd_ref, bd_ref, o_ref):
    i = pl.program_id(0)

    @pl.when(i < nu_ref[0])
    def _():
        x = x_ref[...].astype(BF16)
        gu = _dot(x, wgu_ref[0]) + bgu_ref[0]
        glu = jnp.minimum(gu[:, :D_FF], SWIGLU_LIMIT)
        lin = jnp.clip(gu[:, D_FF:], -SWIGLU_LIMIT, SWIGLU_LIMIT)
        act = glu * jax.nn.sigmoid(SWIGLU_ALPHA * glu) * (lin + 1.0)
        o_ref[...] = _dot(act.astype(BF16), wd_ref[0]) + bd_ref[0]

    @pl.when(i >= nu_ref[0])
    def _():
        o_ref[...] = jnp.zeros(o_ref.shape, o_ref.dtype)


def _experts(blk_expert, n_used, xs, wgu, bgu, wd, bd):
    n_slots = xs.shape[0]
    n_blk = n_slots // MOE_BLOCK
    row = lambda i, be, nu: (i, 0)
    ex3 = lambda i, be, nu: (be[i], 0, 0)
    grid_spec = pltpu.PrefetchScalarGridSpec(
        num_scalar_prefetch=2,
        grid=(n_blk,),
        in_specs=[pl.BlockSpec((MOE_BLOCK, D_MODEL), row),
                  pl.BlockSpec((1, D_MODEL, 2 * D_FF), ex3),
                  pl.BlockSpec((1, 1, 2 * D_FF), ex3),
                  pl.BlockSpec((1, D_FF, D_MODEL), ex3),
                  pl.BlockSpec((1, 1, D_MODEL), ex3)],
        out_specs=pl.BlockSpec((MOE_BLOCK, D_MODEL), row))
    return pl.pallas_call(
        _experts_kernel,
        grid_spec=grid_spec,
        out_shape=jax.ShapeDtypeStruct((n_slots, D_MODEL), F32),
        compiler_params=_cparams("arbitrary"),
        name="moe_experts",
    )(blk_expert, n_used, xs, wgu, bgu.reshape(N_EXPERTS, 1, 2 * D_FF), wd, bd.reshape(N_EXPERTS, 1, D_MODEL))


def _combine_kernel(o0_ref, o1_ref, o2_ref, o3_ref, w_ref, res_ref, gate_ref, fg_ref, out_ref, *, final_norm):
    w = w_ref[...]
    y = w[:, 0:1] * o0_ref[...]
    y = y + w[:, 1:2] * o1_ref[...]
    y = y + w[:, 2:3] * o2_ref[...]
    y = y + w[:, 3:4] * o3_ref[...]
    h = res_ref[...] + gate_ref[0] * y
    if final_norm:
        h = h * lax.rsqrt(jnp.mean(h * h, axis=-1, keepdims=True) + NORM_EPS) * fg_ref[...]
    out_ref[...] = h


def _combine(out4, wts, res, gate, fg, seq, final_norm):
    T = res.shape[0]
    per_b = seq // TM
    nblk = T // TM
    row = lambda i: (i, 0)
    fixed = lambda i: (0, 0)
    pieces = [pl.BlockSpec((TM, D_MODEL), functools.partial(lambda i, k: (k * nblk + i, 0), k=k))
              for k in range(TOP_K)]
    return pl.pallas_call(
        functools.partial(_combine_kernel, final_norm=final_norm),
        grid=(nblk,),
        in_specs=pieces + [pl.BlockSpec((TM, TOP_K), row), pl.BlockSpec((TM, D_MODEL), row),
                           pl.BlockSpec((1, 1, D_MODEL), lambda i: (i // per_b, 0, 0)),
                           pl.BlockSpec((1, D_MODEL), fixed)],
        out_specs=pl.BlockSpec((TM, D_MODEL), row),
        out_shape=jax.ShapeDtypeStruct((T, D_MODEL), F32),
        compiler_params=_cparams("parallel"),
        name="moe_combine",
    )(out4, out4, out4, out4, wts, res, gate, fg)


def _rope_tables(positions):
    inv = ROPE_THETA ** (-jnp.arange(0, 2 * ROT_HALF, 2, dtype=F32) / (2 * ROT_HALF))
    ang = positions.astype(F32)[..., None] * inv
    cos, sin = jnp.cos(ang), jnp.sin(ang)
    lead = cos.shape[:-1]
    ones = jnp.ones(lead + (HEAD_DIM - 2 * ROT_HALF,), F32)
    zeros = jnp.zeros(lead + (HEAD_DIM - 2 * ROT_HALF,), F32)
    z8 = jnp.zeros(lead + (ROT_HALF,), F32)
    rc = jnp.concatenate([cos, cos, ones], -1)
    ra = jnp.concatenate([-sin, z8, zeros], -1)
    rb = jnp.concatenate([z8, sin, zeros], -1)
    two = lambda t: jnp.concatenate([t, t], -1)
    return two(rc), two(ra), two(rb)


def _moe_dispatch(top_i):
    T = top_i.shape[0]
    TK = T * TOP_K
    e = top_i.reshape(TK)
    order = jnp.argsort(e)
    e_sorted = e[order]
    counts = jnp.bincount(e, length=N_EXPERTS)
    start = jnp.cumsum(counts) - counts
    pad_counts = (counts + MOE_BLOCK - 1) // MOE_BLOCK * MOE_BLOCK
    pad_end = jnp.cumsum(pad_counts)
    pad_start = pad_end - pad_counts
    slot_sorted = (pad_start[e_sorted] + jnp.arange(TK) - start[e_sorted]).astype(I32)
    n_slots = TK + N_EXPERTS * MOE_BLOCK
    n_blk = n_slots // MOE_BLOCK
    tok_of_slot = jnp.zeros((n_slots,), I32).at[slot_sorted].set((order // TOP_K).astype(I32))
    slot = jnp.zeros((TK,), I32).at[order].set(slot_sorted).reshape(T, TOP_K)
    blk_start = jnp.arange(n_blk) * MOE_BLOCK
    blk_expert = jnp.minimum(jnp.sum(pad_end[None, :] <= blk_start[:, None], axis=1), N_EXPERTS - 1).astype(I32)
    n_used = (pad_end[-1] // MOE_BLOCK).astype(I32).reshape(1)
    return tok_of_slot, slot, blk_expert, n_used


def _moe_layer(h, g, scale, shift, gate, router_w, router_b, wgu, bgu, wd, bd, fg, seq, final_norm):
    T = h.shape[0]
    rw = jnp.pad(router_w, ((0, 0), (0, LANES - N_EXPERTS)))
    rwh = rw.astype(BF16)
    rwl = (rw - rwh.astype(F32)).astype(BF16)
    rb = jnp.pad(router_b, (0, LANES - N_EXPERTS)).reshape(1, LANES)
    hn, tw = _router(h, g, scale, shift, rwh, rwl, rb, seq)
    top_i = tw[:TOP_K].T.astype(I32)
    wts = tw[TOP_K:].T
    tok_of_slot, slot, blk_expert, n_used = _moe_dispatch(top_i)
    xs = _gather_rows(hn, tok_of_slot)
    outs = _experts(blk_expert, n_used, xs, wgu.astype(BF16), bgu, wd.astype(BF16), bd)
    out4 = _gather_rows(outs, slot.T.reshape(T * TOP_K))
    return _combine(out4, wts, h, gate, fg, seq, final_norm)


def _cover_matrix(seq):
    n_cmp = (seq - CMP_LEN) // CMP_STRIDE + 1
    n_sel = seq // SEL_LEN
    cmp_lo = np.arange(n_cmp) * CMP_STRIDE
    sel_lo = np.arange(n_sel) * SEL_LEN
    cover = ((cmp_lo[:, None] < sel_lo[None, :] + SEL_LEN)
             & (sel_lo[None, :] < cmp_lo[:, None] + CMP_LEN)).astype(np.float32)
    covT = np.zeros((n_sel, seq // CMP_STRIDE), np.float32)
    covT[:, :n_cmp] = cover.T
    return jnp.asarray(covT, BF16)


def kernel(x, c, positions, mod_w, mod_b, norm_g, final_norm_g, a_w_in, a_cmp_pe, a_cmp_w1, a_cmp_w2,
           a_w_out, kv_norm_g, kv_mod_w, kv_mod_b, kv_w, kv_b, b_w_q, b_b_q, b_sinks, b_w_out, b_b_out,
           router_w, router_b, w_gate_up, b_gate_up, w_down, b_down):
    bsz, seq, _ = x.shape
    T = bsz * seq
    G = A_GROUPS
    nq = seq // TQ
    nch = seq // KC
    hd = A_GROUPS * A_QPG * HEAD_DIM
    kd = A_GROUPS * HEAD_DIM

    mods = _mods(c, mod_w.reshape(4, D_MODEL, 3 * D_MODEL), mod_b.reshape(4, 3 * D_MODEL))
    kv_mods = _mods(c, kv_mod_w[None], kv_mod_b[None])[0]

    def split_mod(m, n):
        return [p.reshape(bsz, 1, D_MODEL) for p in jnp.split(m, n, axis=-1)]

    rc, ra, rb = _rope_tables(positions)
    rc2, ra2, rb2 = (t.reshape(T, LANES) for t in (rc, ra, rb))
    h = x.reshape(T, D_MODEL)
    vec = lambda v: v.reshape(1, D_MODEL)

    shift, scale, gate = split_mod(mods[0], 3)
    w_in = a_w_in[0]
    wq = w_in[:, :hd].astype(BF16)
    wkv = w_in[:, hd:hd + 6 * kd].astype(BF16)
    wg = jnp.pad(w_in[:, hd + 6 * kd:], ((0, 0), (0, LANES - 3 * A_GROUPS * A_QPG))).astype(BF16)
    q, kv, gates = _nsa_proj(h, vec(norm_g[0, 0]), scale, shift, wq, wkv, wg, rc2, ra2, rb2, seq)

    qT = q.reshape(bsz, nq, TQ, G, A_QPG, HEAD_DIM).transpose(0, 3, 1, 5, 4, 2).reshape(
        bsz, G, nq, HEAD_DIM, A_QPG * TQ)
    kv6 = kv.reshape(bsz, seq, 6, G, HEAD_DIM)
    craw = kv6[:, :, 0:2].transpose(2, 0, 3, 1, 4).reshape(2, bsz, G, seq // CMP_STRIDE, CMP_STRIDE * HEAD_DIM)
    kmaj = lambda a: a.transpose(0, 2, 1, 3).reshape(bsz, G, nch, KC, HEAD_DIM)
    vmaj = lambda a: a.reshape(bsz, nch, KC, G, HEAD_DIM).transpose(0, 3, 1, 4, 2)
    ks, vsT = kmaj(kv6[:, :, 2]), vmaj(kv6[:, :, 3])
    kw, vwT = kmaj(kv6[:, :, 4]), vmaj(kv6[:, :, 5])
    gT = gates[:, :3 * G * A_QPG].reshape(bsz, nq, TQ, G, A_QPG, 3).transpose(0, 3, 1, 5, 4, 2).reshape(
        bsz, G, nq, 3, A_QPG * TQ)
    gT = jnp.pad(gT, ((0, 0), (0, 0), (0, 0), (0, 5), (0, 0)))

    ncmp = seq // CMP_STRIDE
    end_pos = jnp.minimum(jnp.arange(ncmp) * CMP_STRIDE + CMP_LEN - 1, seq - 1)
    crc, cra, crb = _rope_tables(positions[:, end_pos])
    ident = (jnp.ones_like(crc), jnp.zeros_like(cra), jnp.zeros_like(crb))
    ctab = [jnp.stack([t, i]) for t, i in zip((crc, cra, crb), ident)]
    pe = a_cmp_pe[0].reshape(2, 2, CMP_STRIDE * HEAD_DIM)
    w1 = a_cmp_w1[0].reshape(2, 2, CMP_STRIDE * HEAD_DIM, CMP_HID).astype(BF16)
    w2 = jnp.pad(a_cmp_w2[0], ((0, 0), (0, 0), (0, LANES - HEAD_DIM))).astype(BF16)
    kvc = _compress(craw, pe, w1, w2, *ctab)
    kc = kvc[0]
    vcT = kvc[1].transpose(0, 1, 3, 2)

    o = _nsa_attn(qT, kc, vcT, ks, vsT, kw, vwT, gT, _cover_matrix(seq), seq)
    h = _oproj(o.reshape(T, D_MODEL), a_w_out[0].astype(BF16), jnp.zeros((1, D_MODEL), F32), h, gate, seq)

    shift, scale, gate = split_mod(mods[1], 3)
    h = _moe_layer(h, vec(norm_g[0, 1]), scale, shift, gate, router_w[0], router_b[0], w_gate_up[0],
                   b_gate_up[0], w_down[0], b_down[0], vec(final_norm_g), seq, False)

    kv_shift, kv_scale = split_mod(kv_mods, 2)
    shift, scale, gate = split_mod(mods[2], 3)
    k, v, q = _swa_proj(h, vec(kv_norm_g), kv_scale, kv_shift, vec(norm_g[1, 0]), scale, shift,
                        kv_w.astype(BF16), kv_b.reshape(1, -1), b_w_q[0].astype(BF16), b_b_q[0].reshape(1, -1),
                        rc2, ra2, rb2, seq)
    Gb = B_KV_HEADS
    qT = q.reshape(bsz, nq, TQ, Gb, B_QPG, HEAD_DIM).transpose(0, 3, 1, 5, 4, 2).reshape(
        bsz, Gb, nq, HEAD_DIM, B_QPG * TQ)
    kb = k.reshape(bsz, seq, Gb, HEAD_DIM).transpose(0, 2, 1, 3).reshape(bsz, Gb, nch, KC, HEAD_DIM)
    vbT = v.reshape(bsz, nch, KC, Gb, HEAD_DIM).transpose(0, 3, 1, 4, 2)
    sink_rows = jnp.repeat(b_sinks[0].reshape(Gb, 1, B_QPG), TQ, axis=-1)
    o = _swa_attn(qT, kb, vbT, sink_rows, seq)
    h = _oproj(o.reshape(T, D_MODEL), b_w_out[0].astype(BF16), b_b_out[0].reshape(1, D_MODEL), h, gate, seq)

    shift, scale, gate = split_mod(mods[3], 3)
    out = _moe_layer(h, vec(norm_g[1, 1]), scale, shift, gate, router_w[1], router_b[1], w_gate_up[1],
                     b_gate_up[1], w_down[1], b_down[1], vec(final_norm_g), seq, True)
    return out.reshape(bsz, seq, D_MODEL)
```

```python
import functools

import numpy as np
import jax
import jax.numpy as jnp
from jax import lax
from jax.experimental import pallas as pl
from jax.experimental.pallas import tpu as pltpu

F32 = jnp.float32
BF16 = jnp.bfloat16
I32 = jnp.int32

D_MODEL = 1024
HEAD_DIM = 64
ROT_HALF = 8
ROPE_THETA = 500000.0
NORM_EPS = 1e-5
QK_SCALE = HEAD_DIM ** -0.5
A_GROUPS = 4
A_QPG = 4
CMP_LEN = 32
CMP_STRIDE = 16
CMP_HID = 128
SEL_LEN = 64
SEL_TOPN = 16
A_WINDOW = 512
B_KV_HEADS = 2
B_QPG = 8
B_WINDOW = 128
N_EXPERTS = 32
TOP_K = 4
D_FF = 1024
SWIGLU_LIMIT = 7.0
SWIGLU_ALPHA = 1.702
MOE_BLOCK = 256

LANES = 128
TQ = 128
KC = 128
KS = 512
TM = 512
TD = 256
TC = 256
MASKED = -1e30
M_INIT = -1e29
VMEM_LIMIT = 48 * 1024 * 1024


def _cparams(*sem):
    return pltpu.CompilerParams(dimension_semantics=sem, vmem_limit_bytes=VMEM_LIMIT)


def _dot(a, b):
    return jnp.dot(a, b, preferred_element_type=F32)


def _norm_mod(x, g, scale, shift):
    y = x * lax.rsqrt(jnp.mean(x * x, axis=-1, keepdims=True) + NORM_EPS)
    return (y * g) * (1.0 + scale) + shift


def _rope128(v, rc, ra, rb):
    return v * rc + pltpu.roll(v, LANES - ROT_HALF, 1) * ra + pltpu.roll(v, ROT_HALF, 1) * rb


def _rope(v, rc, ra, rb):
    parts = [_rope128(v[:, s * LANES:(s + 1) * LANES], rc, ra, rb) for s in range(v.shape[1] // LANES)]
    return parts[0] if len(parts) == 1 else jnp.concatenate(parts, axis=1)


def _mods_kernel(c_ref, w_ref, b_ref, o_ref):
    c = c_ref[...]
    ca = c * jax.nn.sigmoid(c)
    o_ref[0] = jnp.dot(ca, w_ref[0], preferred_element_type=F32,
                       precision=lax.Precision.HIGHEST) + b_ref[0]


def _mods(c, w, b):
    n, _, N = w.shape
    bsz = c.shape[0]
    tn = 1024
    return pl.pallas_call(
        _mods_kernel,
        grid=(n, N // tn),
        in_specs=[pl.BlockSpec((bsz, D_MODEL), lambda i, j: (0, 0)),
                  pl.BlockSpec((1, D_MODEL, tn), lambda i, j: (i, 0, j)),
                  pl.BlockSpec((1, 1, tn), lambda i, j: (i, 0, j))],
        out_specs=pl.BlockSpec((1, bsz, tn), lambda i, j: (i, 0, j)),
        out_shape=jax.ShapeDtypeStruct((n, bsz, N), F32),
        compiler_params=_cparams("parallel", "parallel"),
        name="adaln_mods",
    )(c, w, b.reshape(n, 1, N))


def _nsa_proj_kernel(x_ref, g_ref, sc_ref, sh_ref, wq_ref, wkv_ref, wg_ref, rc_ref, ra_ref, rb_ref,
                     q_ref, kv_ref, gate_ref):
    hn = _norm_mod(x_ref[...], g_ref[...], sc_ref[0], sh_ref[0]).astype(BF16)
    rc, ra, rb = rc_ref[...], ra_ref[...], rb_ref[...]
    q = _dot(hn, wq_ref[...])
    q_ref[...] = (_rope(q, rc, ra, rb) * QK_SCALE).astype(BF16)
    kv = _dot(hn, wkv_ref[...])
    kd = A_GROUPS * HEAD_DIM
    for j in range(6):
        piece = kv[:, j * kd:(j + 1) * kd]
        if j in (2, 4):
            piece = _rope(piece, rc, ra, rb)
        kv_ref[:, j * kd:(j + 1) * kd] = piece.astype(BF16)
    gate_ref[...] = jax.nn.sigmoid(_dot(hn, wg_ref[...]))


def _nsa_proj(x, g, scale, shift, wq, wkv, wg, rc, ra, rb, seq):
    T = x.shape[0]
    per_b = seq // TM
    row = lambda i: (i, 0)
    fixed = lambda i: (0, 0)
    bat = lambda i: (i // per_b, 0, 0)
    return pl.pallas_call(
        _nsa_proj_kernel,
        grid=(T // TM,),
        in_specs=[pl.BlockSpec((TM, D_MODEL), row),
                  pl.BlockSpec((1, D_MODEL), fixed),
                  pl.BlockSpec((1, 1, D_MODEL), bat),
                  pl.BlockSpec((1, 1, D_MODEL), bat),
                  pl.BlockSpec(wq.shape, fixed),
                  pl.BlockSpec(wkv.shape, fixed),
                  pl.BlockSpec(wg.shape, fixed),
                  pl.BlockSpec((TM, LANES), row),
                  pl.BlockSpec((TM, LANES), row),
                  pl.BlockSpec((TM, LANES), row)],
        out_specs=[pl.BlockSpec((TM, wq.shape[1]), row),
                   pl.BlockSpec((TM, wkv.shape[1]), row),
                   pl.BlockSpec((TM, LANES), row)],
        out_shape=[jax.ShapeDtypeStruct((T, wq.shape[1]), BF16),
                   jax.ShapeDtypeStruct((T, wkv.shape[1]), BF16),
                   jax.ShapeDtypeStruct((T, LANES), F32)],
        compiler_params=_cparams("parallel"),
        name="nsa_proj",
    )(x, g, scale, shift, wq, wkv, wg, rc, ra, rb)


def _compress_kernel(c_ref, pe_ref, w1_ref, w2_ref, rc_ref, ra_ref, rb_ref, o_ref):
    c = c_ref[0, 0, 0].astype(F32)
    a = _dot((c + pe_ref[0, 0:1, :]).astype(BF16), w1_ref[0, 0])
    b = _dot((c + pe_ref[0, 1:2, :]).astype(BF16), w1_ref[0, 1])
    z = a + pltpu.roll(b, b.shape[0] - 1, 0)
    hid = jax.nn.gelu(z)
    out = _dot(hid.astype(BF16), w2_ref[0])
    out = _rope128(out, rc_ref[0, 0], ra_ref[0, 0], rb_ref[0, 0])
    o_ref[0, 0, 0] = out[:, :HEAD_DIM].astype(BF16)


def _compress(craw, pe, w1, w2, rc, ra, rb):
    _, bsz, G, nch, width = craw.shape
    return pl.pallas_call(
        _compress_kernel,
        grid=(2, bsz, G),
        in_specs=[pl.BlockSpec((1, 1, 1, nch, width), lambda s, b, g: (s, b, g, 0, 0)),
                  pl.BlockSpec((1, 2, width), lambda s, b, g: (s, 0, 0)),
                  pl.BlockSpec((1, 2, width, CMP_HID), lambda s, b, g: (s, 0, 0, 0)),
                  pl.BlockSpec((1, CMP_HID, LANES), lambda s, b, g: (s, 0, 0)),
                  pl.BlockSpec((1, 1, nch, LANES), lambda s, b, g: (s, b, 0, 0)),
                  pl.BlockSpec((1, 1, nch, LANES), lambda s, b, g: (s, b, 0, 0)),
                  pl.BlockSpec((1, 1, nch, LANES), lambda s, b, g: (s, b, 0, 0))],
        out_specs=pl.BlockSpec((1, 1, 1, nch, HEAD_DIM), lambda s, b, g: (s, b, g, 0, 0)),
        out_shape=jax.ShapeDtypeStruct((2, bsz, G, nch, HEAD_DIM), BF16),
        compiler_params=_cparams("parallel", "parallel", "parallel"),
        name="nsa_compress",
    )(craw, pe, w1, w2, rc, ra, rb)


def _flash_step(k, vTs, qT, bias, m, l, acc):
    s = _dot(k, qT) + bias
    m_new = jnp.maximum(m, jnp.max(s, axis=0, keepdims=True))
    alpha = jnp.exp(m - m_new)
    p = jnp.exp(s - m_new)
    l_new = alpha * l + jnp.sum(p, axis=0, keepdims=True)
    acc_new = alpha * acc
    row = 0
    for vT in vTs:
        n = vT.shape[1]
        acc_new = acc_new + _dot(vT, p[row:row + n].astype(BF16))
        row += n
    return m_new, l_new, acc_new


def _tile_lanes(x, n):
    return x if n == 1 else jnp.concatenate([x] * n, axis=1)


def _heads_to_rows(oT, n_heads):
    stacked = jnp.concatenate([oT[:, h * TQ:(h + 1) * TQ] for h in range(n_heads)], axis=0)
    return stacked.T


def _nsa_attn_kernel(qT_ref, kc_ref, vcT_ref, ks_ref, vsT_ref, kw_ref, vwT_ref, gT_ref, covT_ref,
                     o_ref, selb_ref):
    qi = pl.program_id(2)
    H = A_QPG
    W = H * TQ
    qT = qT_ref[0, 0, 0]
    qpos = qi * TQ + lax.broadcasted_iota(I32, (1, TQ), 1)
    ncmp = kc_ref.shape[2]

    n_io = lax.broadcasted_iota(I32, (ncmp, 1), 0)
    end_pos = n_io * CMP_STRIDE + (CMP_LEN - 1)
    valid = (end_pos <= qpos) & (n_io < ncmp - 1)
    bias_c = _tile_lanes(jnp.where(valid, 0.0, MASKED), H)
    s = _dot(kc_ref[0, 0], qT) + bias_c
    m = jnp.maximum(jnp.max(s, axis=0, keepdims=True), M_INIT)
    p = jnp.exp(s - m)
    den = jnp.maximum(jnp.sum(p, axis=0, keepdims=True), jnp.finfo(F32).tiny)
    pn = p / den
    o_c = _dot(vcT_ref[0, 0], pn.astype(BF16))

    psum = pn[:, 0:TQ]
    for h in range(1, H):
        psum = psum + pn[:, h * TQ:(h + 1) * TQ]
    p_hi = psum.astype(BF16)
    p_lo = (psum - p_hi.astype(F32)).astype(BF16)
    covT = covT_ref[...]
    impT = _dot(covT, p_hi) + _dot(covT, p_lo)

    n_sel = covT.shape[0]
    j_io = lax.broadcasted_iota(I32, (n_sel, 1), 0)
    cur = qpos // SEL_LEN
    forced = (j_io == 0) | (j_io == cur) | (j_io == cur - 1)
    causal = j_io <= cur
    score = jnp.where(forced, jnp.inf, jnp.where(causal, impT, -jnp.inf))
    rank = jnp.zeros((n_sel, TQ), I32)
    for i in range(n_sel):
        vi = score[i:i + 1, :]
        ahead = (vi > score) | ((vi == score) & (i < j_io))
        rank = rank + ahead.astype(I32)
    selb = jnp.where(rank < SEL_TOPN, 0.0, MASKED)
    for j in range(n_sel):
        selb_ref[j] = selb[j:j + 1, :]

    init = (jnp.full((1, W), M_INIT, F32), jnp.zeros((1, W), F32), jnp.zeros((HEAD_DIM, W), F32))

    k_io = lax.broadcasted_iota(I32, (KC, 1), 0)
    win_chunks = [qi - A_WINDOW // KC + j for j in range(A_WINDOW // KC + 1)]
    biases = []
    for c in win_chunks:
        kpos = c * KC + k_io
        biases.append(jnp.where((kpos <= qpos) & (kpos > qpos - A_WINDOW) & (kpos >= 0), 0.0, MASKED))
    clamped = [jnp.maximum(c, 0) for c in win_chunks]
    k_w = jnp.concatenate([kw_ref[0, 0, c] for c in clamped], axis=0)
    bias_w = _tile_lanes(jnp.concatenate(biases, axis=0), H)
    _, l_w, acc_w = _flash_step(k_w, [vwT_ref[0, 0, c] for c in clamped], qT, bias_w, *init)

    ks_io = lax.broadcasted_iota(I32, (KS, 1), 0)
    per_step = KS // SEL_LEN

    def sel_body(st, carry):
        kpos = st * KS + ks_io
        blk = jnp.concatenate(
            [jnp.broadcast_to(selb_ref[per_step * st + r], (SEL_LEN, TQ)) for r in range(per_step)], axis=0)
        bias = _tile_lanes(jnp.where(kpos <= qpos, blk, MASKED), H)
        return _flash_step(ks_ref[0, 0, st], [vsT_ref[0, 0, st]], qT, bias, *carry)

    _, l_s, acc_s = lax.fori_loop(0, (qi * TQ) // KS + 1, sel_body, init)

    g = gT_ref[0, 0, 0]
    oT = g[0:1, :] * o_c + g[1:2, :] * (acc_s / l_s) + g[2:3, :] * (acc_w / l_w)
    o_ref[0] = _heads_to_rows(oT, H).astype(BF16)


def _nsa_attn(qT, kc, vcT, ks, vsT, kw, vwT, gT, covT, seq):
    bsz, G, nq = qT.shape[:3]
    W = A_QPG * TQ
    ncmp = kc.shape[2]
    n_sel = covT.shape[0]
    per_bg = lambda b, g, i: (b, g, 0, 0)
    per_bg5 = lambda b, g, i: (b, g, 0, 0, 0)
    return pl.pallas_call(
        _nsa_attn_kernel,
        grid=(bsz, G, nq),
        in_specs=[pl.BlockSpec((1, 1, 1, HEAD_DIM, W), lambda b, g, i: (b, g, i, 0, 0)),
                  pl.BlockSpec((1, 1, ncmp, HEAD_DIM), per_bg),
                  pl.BlockSpec((1, 1, HEAD_DIM, ncmp), per_bg),
                  pl.BlockSpec((1, 1) + ks.shape[2:], per_bg5),
                  pl.BlockSpec((1, 1) + vsT.shape[2:], per_bg5),
                  pl.BlockSpec((1, 1) + kw.shape[2:], per_bg5),
                  pl.BlockSpec((1, 1) + vwT.shape[2:], per_bg5),
                  pl.BlockSpec((1, 1, 1, 8, W), lambda b, g, i: (b, g, i, 0, 0)),
                  pl.BlockSpec((n_sel, ncmp), lambda b, g, i: (0, 0))],
        out_specs=pl.BlockSpec((1, TQ, A_QPG * HEAD_DIM), lambda b, g, i: (b, i, g)),
        out_shape=jax.ShapeDtypeStruct((bsz, seq, D_MODEL), BF16),
        scratch_shapes=[pltpu.VMEM((n_sel, 1, TQ), F32)],
        compiler_params=_cparams("parallel", "parallel", "arbitrary"),
        name="nsa_attention",
    )(qT, kc, vcT, ks, vsT, kw, vwT, gT, covT)


def _swa_attn_kernel(qT_ref, k_ref, vT_ref, sink_ref, o_ref):
    qi = pl.program_id(2)
    H = B_QPG
    W = H * TQ
    half_w = W // 2
    qpos = qi * TQ + lax.broadcasted_iota(I32, (1, TQ), 1)
    k_io = lax.broadcasted_iota(I32, (KC, 1), 0)
    biases = []
    for c in (qi - 1, qi):
        kpos = c * KC + k_io
        biases.append(jnp.where((kpos <= qpos) & (kpos > qpos - B_WINDOW) & (kpos >= 0), 0.0, MASKED))
    bias = _tile_lanes(jnp.concatenate(biases, axis=0), H // 2)
    clamped = (jnp.maximum(qi - 1, 0), qi)
    k = jnp.concatenate([k_ref[0, 0, c] for c in clamped], axis=0)
    vTs = [vT_ref[0, 0, c] for c in clamped]
    outs = []
    for part in range(2):
        lanes = slice(part * half_w, (part + 1) * half_w)
        qT = qT_ref[0, 0, 0][:, lanes]
        m0 = sink_ref[0][:, lanes]
        _, l, acc = _flash_step(k, vTs, qT, bias, m0, jnp.ones((1, half_w), F32),
                                jnp.zeros((HEAD_DIM, half_w), F32))
        outs.append(acc / l)
    oT = jnp.concatenate(outs, axis=1)
    o_ref[0] = _heads_to_rows(oT, H).astype(BF16)


def _swa_attn(qT, k, vT, sink_rows, seq):
    bsz, G, nq = qT.shape[:3]
    W = B_QPG * TQ
    nch = k.shape[2]
    per_bg5 = lambda b, g, i: (b, g, 0, 0, 0)
    return pl.pallas_call(
        _swa_attn_kernel,
        grid=(bsz, G, nq),
        in_specs=[pl.BlockSpec((1, 1, 1, HEAD_DIM, W), lambda b, g, i: (b, g, i, 0, 0)),
                  pl.BlockSpec((1, 1, nch, KC, HEAD_DIM), per_bg5),
                  pl.BlockSpec((1, 1, nch, HEAD_DIM, KC), per_bg5),
                  pl.BlockSpec((1, 1, W), lambda b, g, i: (g, 0, 0))],
        out_specs=pl.BlockSpec((1, TQ, B_QPG * HEAD_DIM), lambda b, g, i: (b, i, g)),
        out_shape=jax.ShapeDtypeStruct((bsz, seq, D_MODEL), BF16),
        compiler_params=_cparams("parallel", "parallel", "arbitrary"),
        name="swa_attention",
    )(qT, k, vT, sink_rows)


def _swa_proj_kernel(x_ref, gkv_ref, sckv_ref, shkv_ref, gq_ref, scq_ref, shq_ref,
                     wkv_ref, bkv_ref, wq_ref, bq_ref, rc_ref, ra_ref, rb_ref, k_ref, v_ref, q_ref):
    x = x_ref[...]
    rc, ra, rb = rc_ref[...], ra_ref[...], rb_ref[...]
    hkv = _norm_mod(x, gkv_ref[...], sckv_ref[0], shkv_ref[0]).astype(BF16)
    kv = _dot(hkv, wkv_ref[...]) + bkv_ref[...]
    kw = B_KV_HEADS * HEAD_DIM
    k_ref[...] = _rope(kv[:, :kw], rc, ra, rb).astype(BF16)
    v_ref[...] = kv[:, kw:].astype(BF16)
    hq = _norm_mod(x, gq_ref[...], scq_ref[0], shq_ref[0]).astype(BF16)
    q = _dot(hq, wq_ref[...]) + bq_ref[...]
    q_ref[...] = (_rope(q, rc, ra, rb) * QK_SCALE).astype(BF16)


def _swa_proj(x, gkv, sckv, shkv, gq, scq, shq, wkv, bkv, wq, bq, rc, ra, rb, seq):
    T = x.shape[0]
    per_b = seq // TM
    row = lambda i: (i, 0)
    fixed = lambda i: (0, 0)
    bat = lambda i: (i // per_b, 0, 0)
    kw = B_KV_HEADS * HEAD_DIM
    vec = pl.BlockSpec((1, D_MODEL), fixed)
    mod = pl.BlockSpec((1, 1, D_MODEL), bat)
    tab = pl.BlockSpec((TM, LANES), row)
    return pl.pallas_call(
        _swa_proj_kernel,
        grid=(T // TM,),
        in_specs=[pl.BlockSpec((TM, D_MODEL), row), vec, mod, mod, vec, mod, mod,
                  pl.BlockSpec(wkv.shape, fixed), pl.BlockSpec(bkv.shape, fixed),
                  pl.BlockSpec(wq.shape, fixed), pl.BlockSpec(bq.shape, fixed), tab, tab, tab],
        out_specs=[pl.BlockSpec((TM, kw), row), pl.BlockSpec((TM, kw), row),
                   pl.BlockSpec((TM, D_MODEL), row)],
        out_shape=[jax.ShapeDtypeStruct((T, kw), BF16), jax.ShapeDtypeStruct((T, kw), BF16),
                   jax.ShapeDtypeStruct((T, D_MODEL), BF16)],
        compiler_params=_cparams("parallel"),
        name="swa_proj",
    )(x, gkv, sckv, shkv, gq, scq, shq, wkv, bkv, wq, bq, rc, ra, rb)


def _oproj_kernel(x_ref, w_ref, b_ref, res_ref, gate_ref, o_ref):
    y = _dot(x_ref[...], w_ref[...]) + b_ref[...]
    o_ref[...] = res_ref[...] + gate_ref[0] * y


def _oproj(x, w, b, res, gate, seq):
    T = x.shape[0]
    per_b = seq // TM
    row = lambda i: (i, 0)
    fixed = lambda i: (0, 0)
    return pl.pallas_call(
        _oproj_kernel,
        grid=(T // TM,),
        in_specs=[pl.BlockSpec((TM, x.shape[1]), row), pl.BlockSpec(w.shape, fixed),
                  pl.BlockSpec((1, D_MODEL), fixed), pl.BlockSpec((TM, D_MODEL), row),
                  pl.BlockSpec((1, 1, D_MODEL), lambda i: (i // per_b, 0, 0))],
        out_specs=pl.BlockSpec((TM, D_MODEL), row),
        out_shape=jax.ShapeDtypeStruct((T, D_MODEL), F32),
        compiler_params=_cparams("parallel"),
        name="out_proj",
    )(x, w, b, res, gate)


def _router_kernel(x_ref, g_ref, sc_ref, sh_ref, wh_ref, wl_ref, b_ref, tri_ref, hn_ref, tw_ref, cnt_ref, run_ref):
    @pl.when(pl.program_id(0) == 0)
    def _():
        run_ref[...] = jnp.zeros(run_ref.shape, F32)

    hn = _norm_mod(x_ref[...], g_ref[...], sc_ref[0], sh_ref[0])
    hn_ref[...] = hn
    hi = hn.astype(BF16)
    lo = (hn - hi.astype(F32)).astype(BF16)
    wh = wh_ref[...]
    logits = _dot(hi, wh) + _dot(lo, wh) + _dot(hi, wl_ref[...]) + b_ref[...]
    col = lax.broadcasted_iota(I32, logits.shape, 1)
    cur = jnp.where(col < N_EXPERTS, logits, -jnp.inf)
    vals, idxs = [], []
    for _ in range(TOP_K):
        mx = jnp.max(cur, axis=-1, keepdims=True)
        ix = jnp.min(jnp.where(cur == mx, col, LANES), axis=-1, keepdims=True)
        vals.append(mx)
        idxs.append(ix)
        cur = jnp.where(col == ix, -jnp.inf, cur)
    es = [jnp.exp(v - vals[0]) for v in vals]
    den = es[0] + es[1] + es[2] + es[3]

    hots = [col == ix for ix in idxs]
    osum = (hots[0] | hots[1] | hots[2] | hots[3]).astype(F32)
    before = _dot(tri_ref[...], osum.astype(BF16)) + run_ref[...]
    ranks = [jnp.sum(jnp.where(hot, before, 0.0), axis=-1, keepdims=True) for hot in hots]
    run_ref[...] = run_ref[...] + jnp.sum(osum, axis=0, keepdims=True)
    cnt_ref[...] = run_ref[...]

    packed = jnp.zeros(logits.shape, F32)
    for k in range(TOP_K):
        packed = jnp.where(col == k, idxs[k].astype(F32), packed)
        packed = jnp.where(col == TOP_K + k, es[k] / den, packed)
        packed = jnp.where(col == 2 * TOP_K + k, ranks[k], packed)
    tw_ref[...] = packed.T[:4 * TOP_K, :]


def _router(x, g, scale, shift, wh, wl, b, seq):
    T = x.shape[0]
    per_b = seq // TM
    row = lambda i: (i, 0)
    fixed = lambda i: (0, 0)
    bat = lambda i: (i // per_b, 0, 0)
    tri = jnp.asarray(np.tril(np.ones((TM, TM), np.float32), -1), BF16)
    return pl.pallas_call(
        _router_kernel,
        grid=(T // TM,),
        in_specs=[pl.BlockSpec((TM, D_MODEL), row), pl.BlockSpec((1, D_MODEL), fixed),
                  pl.BlockSpec((1, 1, D_MODEL), bat), pl.BlockSpec((1, 1, D_MODEL), bat),
                  pl.BlockSpec(wh.shape, fixed), pl.BlockSpec(wl.shape, fixed),
                  pl.BlockSpec((1, LANES), fixed), pl.BlockSpec((TM, TM), fixed)],
        out_specs=[pl.BlockSpec((TM, D_MODEL), row), pl.BlockSpec((4 * TOP_K, TM), lambda i: (0, i)),
                   pl.BlockSpec((1, LANES), fixed)],
        out_shape=[jax.ShapeDtypeStruct((T, D_MODEL), F32),
                   jax.ShapeDtypeStruct((4 * TOP_K, T), F32),
                   jax.ShapeDtypeStruct((1, LANES), F32)],
        scratch_shapes=[pltpu.VMEM((1, LANES), F32)],
        compiler_params=_cparams("arbitrary"),
        name="moe_router",
    )(x, g, scale, shift, wh, wl, b, tri)


def _dispatch_kernel(ids_ref, hn_ref, xs_in_ref, xs_ref, sem):
    del xs_in_ref

    def row_copy(r, slot):
        return pltpu.make_async_copy(hn_ref.at[pl.ds(r, 1)], xs_ref.at[pl.ds(slot, 1)], sem)

    def issue(r, carry):
        for k in range(TOP_K):
            row_copy(r, ids_ref[0, 0, k * TD + r]).start()
        return carry

    lax.fori_loop(0, TD, issue, 0, unroll=8)

    def drain(r, carry):
        for k in range(TOP_K):
            row_copy(r, ids_ref[0, 0, k * TD + r]).wait()
        return carry

    lax.fori_loop(0, TD, drain, 0, unroll=8)


def _dispatch(hn, slot_tiles, xs_zero):
    T = hn.shape[0]
    return pl.pallas_call(
        _dispatch_kernel,
        grid=(T // TD,),
        in_specs=[pl.BlockSpec((1, 1, TOP_K * TD), lambda i: (i, 0, 0), memory_space=pltpu.SMEM),
                  pl.BlockSpec((TD, D_MODEL), lambda i: (i, 0)),
                  pl.BlockSpec(memory_space=pl.ANY)],
        out_specs=pl.BlockSpec(memory_space=pl.ANY),
        out_shape=jax.ShapeDtypeStruct(xs_zero.shape, xs_zero.dtype),
        input_output_aliases={2: 0},
        scratch_shapes=[pltpu.SemaphoreType.DMA(())],
        compiler_params=_cparams("arbitrary"),
        name="moe_dispatch",
    )(slot_tiles, hn, xs_zero)


def _experts_kernel(be_ref, nu_ref, x_ref, wgu_ref, bgu_ref, wd_ref, bd_ref, o_ref):
    i = pl.program_id(0)

    @pl.when(i < nu_ref[0])
    def _():
        x = x_ref[...].astype(BF16)
        gu = _dot(x, wgu_ref[0]) + bgu_ref[0]
        glu = jnp.minimum(gu[:, :D_FF], SWIGLU_LIMIT)
        lin = jnp.clip(gu[:, D_FF:], -SWIGLU_LIMIT, SWIGLU_LIMIT)
        act = glu * jax.nn.sigmoid(SWIGLU_ALPHA * glu) * (lin + 1.0)
        o_ref[...] = _dot(act.astype(BF16), wd_ref[0]) + bd_ref[0]

    @pl.when(i >= nu_ref[0])
    def _():
        o_ref[...] = jnp.zeros(o_ref.shape, o_ref.dtype)


def _experts(blk_expert, n_used, xs, wgu, bgu, wd, bd):
    n_slots = xs.shape[0]
    n_blk = n_slots // MOE_BLOCK
    row = lambda i, be, nu: (i, 0)
    ex3 = lambda i, be, nu: (be[i], 0, 0)
    grid_spec = pltpu.PrefetchScalarGridSpec(
        num_scalar_prefetch=2,
        grid=(n_blk,),
        in_specs=[pl.BlockSpec((MOE_BLOCK, D_MODEL), row),
                  pl.BlockSpec((1, D_MODEL, 2 * D_FF), ex3),
                  pl.BlockSpec((1, 1, 2 * D_FF), ex3),
                  pl.BlockSpec((1, D_FF, D_MODEL), ex3),
                  pl.BlockSpec((1, 1, D_MODEL), ex3)],
        out_specs=pl.BlockSpec((MOE_BLOCK, D_MODEL), row))
    return pl.pallas_call(
        _experts_kernel,
        grid_spec=grid_spec,
        out_shape=jax.ShapeDtypeStruct((n_slots, D_MODEL), F32),
        compiler_params=_cparams("arbitrary"),
        name="moe_experts",
    )(blk_expert, n_used, xs, wgu, bgu.reshape(N_EXPERTS, 1, 2 * D_FF), wd, bd.reshape(N_EXPERTS, 1, D_MODEL))


def _combine_kernel(ids_ref, nxt_ref, outs_ref, w_ref, res_ref, gate_ref, fg_ref, out_ref, buf, sem, *, final_norm):
    i = pl.program_id(0)
    n = pl.num_programs(0)

    def row_copy(ids, r, k, par):
        return pltpu.make_async_copy(outs_ref.at[pl.ds(ids[0, 0, k * TC + r], 1)],
                                     buf.at[par, k, pl.ds(r, 1)], sem.at[par])

    def issue(ids, par):
        def body(r, carry):
            for k in range(TOP_K):
                row_copy(ids, r, k, par).start()
            return carry
        lax.fori_loop(0, TC, body, 0, unroll=8)

    @pl.when(i == 0)
    def _():
        issue(ids_ref, 0)

    @pl.when(i + 1 < n)
    def _():
        issue(nxt_ref, (i + 1) % 2)

    par = i % 2

    def drain(r, carry):
        for k in range(TOP_K):
            row_copy(ids_ref, r, k, par).wait()
        return carry

    lax.fori_loop(0, TC, drain, 0, unroll=8)

    w = w_ref[...]
    y = w[:, 0:1] * buf[par, 0]
    for k in range(1, TOP_K):
        y = y + w[:, k:k + 1] * buf[par, k]
    h = res_ref[...] + gate_ref[0] * y
    if final_norm:
        h = h * lax.rsqrt(jnp.mean(h * h, axis=-1, keepdims=True) + NORM_EPS) * fg_ref[...]
    out_ref[...] = h


def _combine(outs, slot_tiles, wts, res, gate, fg, seq, final_norm):
    T = res.shape[0]
    per_b = seq // TC
    nblk = T // TC
    row = lambda i: (i, 0)
    fixed = lambda i: (0, 0)
    ids = lambda i: (i, 0, 0)
    nxt = lambda i: (jnp.minimum(i + 1, nblk - 1), 0, 0)
    return pl.pallas_call(
        functools.partial(_combine_kernel, final_norm=final_norm),
        grid=(nblk,),
        in_specs=[pl.BlockSpec((1, 1, TOP_K * TC), ids, memory_space=pltpu.SMEM),
                  pl.BlockSpec((1, 1, TOP_K * TC), nxt, memory_space=pltpu.SMEM),
                  pl.BlockSpec(memory_space=pl.ANY),
                  pl.BlockSpec((TC, TOP_K), row), pl.BlockSpec((TC, D_MODEL), row),
                  pl.BlockSpec((1, 1, D_MODEL), lambda i: (i // per_b, 0, 0)),
                  pl.BlockSpec((1, D_MODEL), fixed)],
        out_specs=pl.BlockSpec((TC, D_MODEL), row),
        out_shape=jax.ShapeDtypeStruct((T, D_MODEL), F32),
        scratch_shapes=[pltpu.VMEM((2, TOP_K, TC, D_MODEL), F32), pltpu.SemaphoreType.DMA((2,))],
        compiler_params=_cparams("arbitrary"),
        name="moe_combine",
    )(slot_tiles, slot_tiles, outs, wts, res, gate, fg)


def _rope_tables(positions):
    inv = ROPE_THETA ** (-jnp.arange(0, 2 * ROT_HALF, 2, dtype=F32) / (2 * ROT_HALF))
    ang = positions.astype(F32)[..., None] * inv
    cos, sin = jnp.cos(ang), jnp.sin(ang)
    lead = cos.shape[:-1]
    ones = jnp.ones(lead + (HEAD_DIM - 2 * ROT_HALF,), F32)
    zeros = jnp.zeros(lead + (HEAD_DIM - 2 * ROT_HALF,), F32)
    z8 = jnp.zeros(lead + (ROT_HALF,), F32)
    rc = jnp.concatenate([cos, cos, ones], -1)
    ra = jnp.concatenate([-sin, z8, zeros], -1)
    rb = jnp.concatenate([z8, sin, zeros], -1)
    two = lambda t: jnp.concatenate([t, t], -1)
    return two(rc), two(ra), two(rb)


def _moe_slots(expert, rank, counts, n_tok):
    pad_counts = (counts + MOE_BLOCK - 1) // MOE_BLOCK * MOE_BLOCK
    pad_end = jnp.cumsum(pad_counts)
    pad_start = pad_end - pad_counts
    slot = pad_start[expert] + rank
    n_blk = (n_tok * TOP_K + N_EXPERTS * MOE_BLOCK) // MOE_BLOCK
    blk_start = jnp.arange(n_blk) * MOE_BLOCK
    blk_expert = jnp.minimum(jnp.sum(pad_end[None, :] <= blk_start[:, None], axis=1), N_EXPERTS - 1).astype(I32)
    n_used = (pad_end[-1] // MOE_BLOCK).astype(I32).reshape(1)
    return slot.astype(I32), blk_expert, n_used


def _slot_tiles(slot, tile):
    n_tok = slot.shape[1]
    return slot.reshape(TOP_K, n_tok // tile, tile).transpose(1, 0, 2).reshape(n_tok // tile, 1, TOP_K * tile)


def _moe_layer(h, g, scale, shift, gate, router_w, router_b, wgu, bgu, wd, bd, fg, seq, final_norm):
    T = h.shape[0]
    rw = jnp.pad(router_w, ((0, 0), (0, LANES - N_EXPERTS)))
    rwh = rw.astype(BF16)
    rwl = (rw - rwh.astype(F32)).astype(BF16)
    rb = jnp.pad(router_b, (0, LANES - N_EXPERTS)).reshape(1, LANES)
    hn, tw, cnt = _router(h, g, scale, shift, rwh, rwl, rb, seq)
    expert = tw[:TOP_K].astype(I32)
    wts = tw[TOP_K:2 * TOP_K].T
    rank = tw[2 * TOP_K:3 * TOP_K].astype(I32)
    slot, blk_expert, n_used = _moe_slots(expert, rank, cnt[0, :N_EXPERTS].astype(I32), T)
    n_slots = T * TOP_K + N_EXPERTS * MOE_BLOCK
    xs = _dispatch(hn, _slot_tiles(slot, TD), jnp.zeros((n_slots, D_MODEL), F32))
    outs = _experts(blk_expert, n_used, xs, wgu.astype(BF16), bgu, wd.astype(BF16), bd)
    return _combine(outs, _slot_tiles(slot, TC), wts, h, gate, fg, seq, final_norm)


def _cover_matrix(seq):
    n_cmp = (seq - CMP_LEN) // CMP_STRIDE + 1
    n_sel = seq // SEL_LEN
    cmp_lo = np.arange(n_cmp) * CMP_STRIDE
    sel_lo = np.arange(n_sel) * SEL_LEN
    cover = ((cmp_lo[:, None] < sel_lo[None, :] + SEL_LEN)
             & (sel_lo[None, :] < cmp_lo[:, None] + CMP_LEN)).astype(np.float32)
    covT = np.zeros((n_sel, seq // CMP_STRIDE), np.float32)
    covT[:, :n_cmp] = cover.T
    return jnp.asarray(covT, BF16)


def kernel(x, c, positions, mod_w, mod_b, norm_g, final_norm_g, a_w_in, a_cmp_pe, a_cmp_w1, a_cmp_w2,
           a_w_out, kv_norm_g, kv_mod_w, kv_mod_b, kv_w, kv_b, b_w_q, b_b_q, b_sinks, b_w_out, b_b_out,
           router_w, router_b, w_gate_up, b_gate_up, w_down, b_down):
    bsz, seq, _ = x.shape
    T = bsz * seq
    G = A_GROUPS
    nq = seq // TQ
    nch = seq // KC
    hd = A_GROUPS * A_QPG * HEAD_DIM
    kd = A_GROUPS * HEAD_DIM

    mods = _mods(c, mod_w.reshape(4, D_MODEL, 3 * D_MODEL), mod_b.reshape(4, 3 * D_MODEL))
    kv_mods = _mods(c, kv_mod_w[None], kv_mod_b[None])[0]

    def split_mod(m, n):
        return [p.reshape(bsz, 1, D_MODEL) for p in jnp.split(m, n, axis=-1)]

    rc, ra, rb = _rope_tables(positions)
    rc2, ra2, rb2 = (t.reshape(T, LANES) for t in (rc, ra, rb))
    h = x.reshape(T, D_MODEL)
    vec = lambda v: v.reshape(1, D_MODEL)

    shift, scale, gate = split_mod(mods[0], 3)
    w_in = a_w_in[0]
    wq = w_in[:, :hd].astype(BF16)
    wkv = w_in[:, hd:hd + 6 * kd].astype(BF16)
    wg = jnp.pad(w_in[:, hd + 6 * kd:], ((0, 0), (0, LANES - 3 * A_GROUPS * A_QPG))).astype(BF16)
    q, kv, gates = _nsa_proj(h, vec(norm_g[0, 0]), scale, shift, wq, wkv, wg, rc2, ra2, rb2, seq)

    qT = q.reshape(bsz, nq, TQ, G, A_QPG, HEAD_DIM).transpose(0, 3, 1, 5, 4, 2).reshape(
        bsz, G, nq, HEAD_DIM, A_QPG * TQ)
    kv6 = kv.reshape(bsz, seq, 6, G, HEAD_DIM)
    craw = kv6[:, :, 0:2].transpose(2, 0, 3, 1, 4).reshape(2, bsz, G, seq // CMP_STRIDE, CMP_STRIDE * HEAD_DIM)
    kmaj = lambda a, ch: a.transpose(0, 2, 1, 3).reshape(bsz, G, seq // ch, ch, HEAD_DIM)
    vmaj = lambda a, ch: a.reshape(bsz, seq // ch, ch, G, HEAD_DIM).transpose(0, 3, 1, 4, 2)
    ks, vsT = kmaj(kv6[:, :, 2], KS), vmaj(kv6[:, :, 3], KS)
    kw, vwT = kmaj(kv6[:, :, 4], KC), vmaj(kv6[:, :, 5], KC)
    gT = gates[:, :3 * G * A_QPG].reshape(bsz, nq, TQ, G, A_QPG, 3).transpose(0, 3, 1, 5, 4, 2).reshape(
        bsz, G, nq, 3, A_QPG * TQ)
    gT = jnp.pad(gT, ((0, 0), (0, 0), (0, 0), (0, 5), (0, 0)))

    ncmp = seq // CMP_STRIDE
    end_pos = jnp.minimum(jnp.arange(ncmp) * CMP_STRIDE + CMP_LEN - 1, seq - 1)
    crc, cra, crb = _rope_tables(positions[:, end_pos])
    ident = (jnp.ones_like(crc), jnp.zeros_like(cra), jnp.zeros_like(crb))
    ctab = [jnp.stack([t, i]) for t, i in zip((crc, cra, crb), ident)]
    pe = a_cmp_pe[0].reshape(2, 2, CMP_STRIDE * HEAD_DIM)
    w1 = a_cmp_w1[0].reshape(2, 2, CMP_STRIDE * HEAD_DIM, CMP_HID).astype(BF16)
    w2 = jnp.pad(a_cmp_w2[0], ((0, 0), (0, 0), (0, LANES - HEAD_DIM))).astype(BF16)
    kvc = _compress(craw, pe, w1, w2, *ctab)
    kc = kvc[0]
    vcT = kvc[1].transpose(0, 1, 3, 2)

    o = _nsa_attn(qT, kc, vcT, ks, vsT, kw, vwT, gT, _cover_matrix(seq), seq)
    h = _oproj(o.reshape(T, D_MODEL), a_w_out[0].astype(BF16), jnp.zeros((1, D_MODEL), F32), h, gate, seq)

    shift, scale, gate = split_mod(mods[1], 3)
    h = _moe_layer(h, vec(norm_g[0, 1]), scale, shift, gate, router_w[0], router_b[0], w_gate_up[0],
                   b_gate_up[0], w_down[0], b_down[0], vec(final_norm_g), seq, False)

    kv_shift, kv_scale = split_mod(kv_mods, 2)
    shift, scale, gate = split_mod(mods[2], 3)
    k, v, q = _swa_proj(h, vec(kv_norm_g), kv_scale, kv_shift, vec(norm_g[1, 0]), scale, shift,
                        kv_w.astype(BF16), kv_b.reshape(1, -1), b_w_q[0].astype(BF16), b_b_q[0].reshape(1, -1),
                        rc2, ra2, rb2, seq)
    Gb = B_KV_HEADS
    qT = q.reshape(bsz, nq, TQ, Gb, B_QPG, HEAD_DIM).transpose(0, 3, 1, 5, 4, 2).reshape(
        bsz, Gb, nq, HEAD_DIM, B_QPG * TQ)
    kb = k.reshape(bsz, seq, Gb, HEAD_DIM).transpose(0, 2, 1, 3).reshape(bsz, Gb, nch, KC, HEAD_DIM)
    vbT = v.reshape(bsz, nch, KC, Gb, HEAD_DIM).transpose(0, 3, 1, 4, 2)
    sink_rows = jnp.repeat(b_sinks[0].reshape(Gb, 1, B_QPG), TQ, axis=-1)
    o = _swa_attn(qT, kb, vbT, sink_rows, seq)
    h = _oproj(o.reshape(T, D_MODEL), b_w_out[0].astype(BF16), b_b_out[0].reshape(1, D_MODEL), h, gate, seq)

    shift, scale, gate = split_mod(mods[3], 3)
    out = _moe_layer(h, vec(norm_g[1, 1]), scale, shift, gate, router_w[1], router_b[1], w_gate_up[1],
                     b_gate_up[1], w_down[1], b_down[1], vec(final_norm_g), seq, True)
    return out.reshape(bsz, seq, D_MODEL)
```

```python
import functools

import numpy as np
import jax
import jax.numpy as jnp
from jax import lax
from jax.experimental import pallas as pl
from jax.experimental.pallas import tpu as pltpu

F32 = jnp.float32
BF16 = jnp.bfloat16
I32 = jnp.int32

D_MODEL = 1024
HEAD_DIM = 64
ROT_HALF = 8
ROPE_THETA = 500000.0
NORM_EPS = 1e-5
QK_SCALE = HEAD_DIM ** -0.5
A_GROUPS = 4
A_QPG = 4
CMP_LEN = 32
CMP_STRIDE = 16
CMP_HID = 128
SEL_LEN = 64
SEL_TOPN = 16
A_WINDOW = 512
B_KV_HEADS = 2
B_QPG = 8
B_WINDOW = 128
N_EXPERTS = 32
TOP_K = 4
D_FF = 1024
SWIGLU_LIMIT = 7.0
SWIGLU_ALPHA = 1.702
MOE_BLOCK = 256

LANES = 128
TQ = 256
KC = 128
KS = 512
CHAIN_LANES = 1024
TM = 512
TD = 256
TC = 256
MASKED = -1e30
M_INIT = -1e29
VMEM_LIMIT = 48 * 1024 * 1024
EXPERTS_VMEM_LIMIT = 56 * 1024 * 1024


def _cparams(*sem):
    return pltpu.CompilerParams(dimension_semantics=sem, vmem_limit_bytes=VMEM_LIMIT)


def _dot(a, b):
    return jnp.dot(a, b, preferred_element_type=F32)


def _norm_mod(x, g, scale, shift):
    y = x * lax.rsqrt(jnp.mean(x * x, axis=-1, keepdims=True) + NORM_EPS)
    return (y * g) * (1.0 + scale) + shift


def _rope128(v, rc, ra, rb):
    return v * rc + pltpu.roll(v, LANES - ROT_HALF, 1) * ra + pltpu.roll(v, ROT_HALF, 1) * rb


def _rope(v, rc, ra, rb):
    parts = [_rope128(v[:, s * LANES:(s + 1) * LANES], rc, ra, rb) for s in range(v.shape[1] // LANES)]
    return parts[0] if len(parts) == 1 else jnp.concatenate(parts, axis=1)


def _mods_kernel(c_ref, w_ref, b_ref, o_ref):
    c = c_ref[...]
    ca = c * jax.nn.sigmoid(c)
    o_ref[0] = jnp.dot(ca, w_ref[0], preferred_element_type=F32,
                       precision=lax.Precision.HIGHEST) + b_ref[0]


def _mods(c, w, b):
    n, _, N = w.shape
    bsz = c.shape[0]
    tn = 1024
    return pl.pallas_call(
        _mods_kernel,
        grid=(n, N // tn),
        in_specs=[pl.BlockSpec((bsz, D_MODEL), lambda i, j: (0, 0)),
                  pl.BlockSpec((1, D_MODEL, tn), lambda i, j: (i, 0, j)),
                  pl.BlockSpec((1, 1, tn), lambda i, j: (i, 0, j))],
        out_specs=pl.BlockSpec((1, bsz, tn), lambda i, j: (i, 0, j)),
        out_shape=jax.ShapeDtypeStruct((n, bsz, N), F32),
        compiler_params=_cparams("parallel", "parallel"),
        name="adaln_mods",
    )(c, w, b.reshape(n, 1, N))


def _nsa_proj_kernel(x_ref, g_ref, sc_ref, sh_ref, wq_ref, wkv_ref, wg_ref, rc_ref, ra_ref, rb_ref,
                     q_ref, kv_ref, gate_ref):
    hn = _norm_mod(x_ref[...], g_ref[...], sc_ref[0], sh_ref[0]).astype(BF16)
    rc, ra, rb = rc_ref[...], ra_ref[...], rb_ref[...]
    q = _dot(hn, wq_ref[...])
    q_ref[...] = (_rope(q, rc, ra, rb) * QK_SCALE).astype(BF16)
    kv = _dot(hn, wkv_ref[...])
    kd = A_GROUPS * HEAD_DIM
    for j in range(6):
        piece = kv[:, j * kd:(j + 1) * kd]
        if j in (2, 4):
            piece = _rope(piece, rc, ra, rb)
        kv_ref[:, j * kd:(j + 1) * kd] = piece.astype(BF16)
    gate_ref[...] = jax.nn.sigmoid(_dot(hn, wg_ref[...]))


def _nsa_proj(x, g, scale, shift, wq, wkv, wg, rc, ra, rb, seq):
    T = x.shape[0]
    per_b = seq // TM
    row = lambda i: (i, 0)
    fixed = lambda i: (0, 0)
    bat = lambda i: (i // per_b, 0, 0)
    return pl.pallas_call(
        _nsa_proj_kernel,
        grid=(T // TM,),
        in_specs=[pl.BlockSpec((TM, D_MODEL), row),
                  pl.BlockSpec((1, D_MODEL), fixed),
                  pl.BlockSpec((1, 1, D_MODEL), bat),
                  pl.BlockSpec((1, 1, D_MODEL), bat),
                  pl.BlockSpec(wq.shape, fixed),
                  pl.BlockSpec(wkv.shape, fixed),
                  pl.BlockSpec(wg.shape, fixed),
                  pl.BlockSpec((TM, LANES), row),
                  pl.BlockSpec((TM, LANES), row),
                  pl.BlockSpec((TM, LANES), row)],
        out_specs=[pl.BlockSpec((TM, wq.shape[1]), row),
                   pl.BlockSpec((TM, wkv.shape[1]), row),
                   pl.BlockSpec((TM, LANES), row)],
        out_shape=[jax.ShapeDtypeStruct((T, wq.shape[1]), BF16),
                   jax.ShapeDtypeStruct((T, wkv.shape[1]), BF16),
                   jax.ShapeDtypeStruct((T, LANES), F32)],
        compiler_params=_cparams("parallel"),
        name="nsa_proj",
    )(x, g, scale, shift, wq, wkv, wg, rc, ra, rb)


def _compress_kernel(c_ref, pe_ref, w1_ref, w2_ref, rc_ref, ra_ref, rb_ref, o_ref):
    c = c_ref[0, 0, 0].astype(F32)
    a = _dot((c + pe_ref[0, 0:1, :]).astype(BF16), w1_ref[0, 0])
    b = _dot((c + pe_ref[0, 1:2, :]).astype(BF16), w1_ref[0, 1])
    z = a + pltpu.roll(b, b.shape[0] - 1, 0)
    hid = jax.nn.gelu(z)
    out = _dot(hid.astype(BF16), w2_ref[0])
    out = _rope128(out, rc_ref[0, 0], ra_ref[0, 0], rb_ref[0, 0])
    o_ref[0, 0, 0] = out[:, :HEAD_DIM].astype(BF16)


def _compress(craw, pe, w1, w2, rc, ra, rb):
    _, bsz, G, nch, width = craw.shape
    return pl.pallas_call(
        _compress_kernel,
        grid=(2, bsz, G),
        in_specs=[pl.BlockSpec((1, 1, 1, nch, width), lambda s, b, g: (s, b, g, 0, 0)),
                  pl.BlockSpec((1, 2, width), lambda s, b, g: (s, 0, 0)),
                  pl.BlockSpec((1, 2, width, CMP_HID), lambda s, b, g: (s, 0, 0, 0)),
                  pl.BlockSpec((1, CMP_HID, LANES), lambda s, b, g: (s, 0, 0)),
                  pl.BlockSpec((1, 1, nch, LANES), lambda s, b, g: (s, b, 0, 0)),
                  pl.BlockSpec((1, 1, nch, LANES), lambda s, b, g: (s, b, 0, 0)),
                  pl.BlockSpec((1, 1, nch, LANES), lambda s, b, g: (s, b, 0, 0))],
        out_specs=pl.BlockSpec((1, 1, 1, nch, HEAD_DIM), lambda s, b, g: (s, b, g, 0, 0)),
        out_shape=jax.ShapeDtypeStruct((2, bsz, G, nch, HEAD_DIM), BF16),
        compiler_params=_cparams("parallel", "parallel", "parallel"),
        name="nsa_compress",
    )(craw, pe, w1, w2, rc, ra, rb)


def _flash_step(k, vTs, qT, bias, m, l, acc):
    outs = []
    for g0 in range(0, qT.shape[1], CHAIN_LANES):
        lanes = slice(g0, g0 + CHAIN_LANES)
        s = _dot(k, qT[:, lanes]) + bias[:, lanes]
        m_new = jnp.maximum(m[:, lanes], jnp.max(s, axis=0, keepdims=True))
        alpha = jnp.exp(m[:, lanes] - m_new)
        p = jnp.exp(s - m_new)
        l_new = alpha * l[:, lanes] + jnp.sum(p, axis=0, keepdims=True)
        acc_new = alpha * acc[:, lanes]
        row = 0
        for vT in vTs:
            n = vT.shape[1]
            acc_new = acc_new + _dot(vT, p[row:row + n].astype(BF16))
            row += n
        outs.append((m_new, l_new, acc_new))
    if len(outs) == 1:
        return outs[0]
    return tuple(jnp.concatenate([o[j] for o in outs], axis=1) for j in range(3))


def _tile_lanes(x, n):
    return x if n == 1 else jnp.concatenate([x] * n, axis=1)


def _heads_to_rows(oT, n_heads):
    stacked = jnp.concatenate([oT[:, h * TQ:(h + 1) * TQ] for h in range(n_heads)], axis=0)
    return stacked.T


def _nsa_attn_kernel(qT_ref, kc_ref, vcT_ref, ks_ref, vsT_ref, kw_ref, vwT_ref, gT_ref, covT_ref,
                     o_ref, selb_ref):
    qi = pl.program_id(2)
    H = A_QPG
    W = H * TQ
    qT = qT_ref[0, 0, 0]
    qpos = qi * TQ + lax.broadcasted_iota(I32, (1, TQ), 1)
    ncmp = kc_ref.shape[2]

    n_io = lax.broadcasted_iota(I32, (ncmp, 1), 0)
    end_pos = n_io * CMP_STRIDE + (CMP_LEN - 1)
    valid = (end_pos <= qpos) & (n_io < ncmp - 1)
    bias_c = _tile_lanes(jnp.where(valid, 0.0, MASKED), H)
    s = _dot(kc_ref[0, 0], qT) + bias_c
    m = jnp.maximum(jnp.max(s, axis=0, keepdims=True), M_INIT)
    p = jnp.exp(s - m)
    den = jnp.maximum(jnp.sum(p, axis=0, keepdims=True), jnp.finfo(F32).tiny)
    pn = p / den
    o_c = _dot(vcT_ref[0, 0], pn.astype(BF16))

    psum = pn[:, 0:TQ]
    for h in range(1, H):
        psum = psum + pn[:, h * TQ:(h + 1) * TQ]
    p_hi = psum.astype(BF16)
    p_lo = (psum - p_hi.astype(F32)).astype(BF16)
    covT = covT_ref[...]
    impT = _dot(covT, p_hi) + _dot(covT, p_lo)

    n_sel = covT.shape[0]
    j_io = lax.broadcasted_iota(I32, (n_sel, 1), 0)
    cur = qpos // SEL_LEN
    forced = (j_io == 0) | (j_io == cur) | (j_io == cur - 1)
    causal = j_io <= cur
    score = jnp.where(forced, jnp.inf, jnp.where(causal, impT, -jnp.inf))
    rank = jnp.zeros((n_sel, TQ), I32)
    for i in range(n_sel):
        vi = score[i:i + 1, :]
        ahead = (vi > score) | ((vi == score) & (i < j_io))
        rank = rank + ahead.astype(I32)
    selb = jnp.where(rank < SEL_TOPN, 0.0, MASKED)
    for j in range(n_sel):
        selb_ref[j] = selb[j:j + 1, :]

    init = (jnp.full((1, W), M_INIT, F32), jnp.zeros((1, W), F32), jnp.zeros((HEAD_DIM, W), F32))

    k_io = lax.broadcasted_iota(I32, (KC, 1), 0)
    win_chunks = [qi * (TQ // KC) - A_WINDOW // KC + j for j in range((A_WINDOW + TQ) // KC)]
    biases = []
    for c in win_chunks:
        kpos = c * KC + k_io
        biases.append(jnp.where((kpos <= qpos) & (kpos > qpos - A_WINDOW) & (kpos >= 0), 0.0, MASKED))
    clamped = [jnp.maximum(c, 0) for c in win_chunks]
    k_w = jnp.concatenate([kw_ref[0, 0, c] for c in clamped], axis=0)
    bias_w = _tile_lanes(jnp.concatenate(biases, axis=0), H)
    _, l_w, acc_w = _flash_step(k_w, [vwT_ref[0, 0, c] for c in clamped], qT, bias_w, *init)

    ks_io = lax.broadcasted_iota(I32, (KS, 1), 0)
    per_step = KS // SEL_LEN

    def sel_body(st, carry):
        kpos = st * KS + ks_io
        blk = jnp.concatenate(
            [jnp.broadcast_to(selb_ref[per_step * st + r], (SEL_LEN, TQ)) for r in range(per_step)], axis=0)
        bias = _tile_lanes(jnp.where(kpos <= qpos, blk, MASKED), H)
        return _flash_step(ks_ref[0, 0, st], [vsT_ref[0, 0, st]], qT, bias, *carry)

    _, l_s, acc_s = lax.fori_loop(0, (qi * TQ + TQ - 1) // KS + 1, sel_body, init)

    g = gT_ref[0, 0, 0]
    oT = g[0:1, :] * o_c + g[1:2, :] * (acc_s / l_s) + g[2:3, :] * (acc_w / l_w)
    o_ref[0] = _heads_to_rows(oT, H).astype(BF16)


def _nsa_attn(qT, kc, vcT, ks, vsT, kw, vwT, gT, covT, seq):
    bsz, G, nq = qT.shape[:3]
    W = A_QPG * TQ
    ncmp = kc.shape[2]
    n_sel = covT.shape[0]
    per_bg = lambda b, g, i: (b, g, 0, 0)
    per_bg5 = lambda b, g, i: (b, g, 0, 0, 0)
    return pl.pallas_call(
        _nsa_attn_kernel,
        grid=(bsz, G, nq),
        in_specs=[pl.BlockSpec((1, 1, 1, HEAD_DIM, W), lambda b, g, i: (b, g, i, 0, 0)),
                  pl.BlockSpec((1, 1, ncmp, HEAD_DIM), per_bg),
                  pl.BlockSpec((1, 1, HEAD_DIM, ncmp), per_bg),
                  pl.BlockSpec((1, 1) + ks.shape[2:], per_bg5),
                  pl.BlockSpec((1, 1) + vsT.shape[2:], per_bg5),
                  pl.BlockSpec((1, 1) + kw.shape[2:], per_bg5),
                  pl.BlockSpec((1, 1) + vwT.shape[2:], per_bg5),
                  pl.BlockSpec((1, 1, 1, 8, W), lambda b, g, i: (b, g, i, 0, 0)),
                  pl.BlockSpec((n_sel, ncmp), lambda b, g, i: (0, 0))],
        out_specs=pl.BlockSpec((1, TQ, A_QPG * HEAD_DIM), lambda b, g, i: (b, i, g)),
        out_shape=jax.ShapeDtypeStruct((bsz, seq, D_MODEL), BF16),
        scratch_shapes=[pltpu.VMEM((n_sel, 1, TQ), F32)],
        compiler_params=_cparams("parallel", "parallel", "arbitrary"),
        name="nsa_attention",
    )(qT, kc, vcT, ks, vsT, kw, vwT, gT, covT)


def _swa_attn_kernel(qT_ref, k_ref, vT_ref, sink_ref, o_ref):
    qi = pl.program_id(2)
    H = B_QPG
    W = H * TQ
    qpos = qi * TQ + lax.broadcasted_iota(I32, (1, TQ), 1)
    k_io = lax.broadcasted_iota(I32, (KC, 1), 0)
    chunks = [qi * (TQ // KC) - B_WINDOW // KC + j for j in range((B_WINDOW + TQ) // KC)]
    biases = []
    for c in chunks:
        kpos = c * KC + k_io
        biases.append(jnp.where((kpos <= qpos) & (kpos > qpos - B_WINDOW) & (kpos >= 0), 0.0, MASKED))
    bias = _tile_lanes(jnp.concatenate(biases, axis=0), H)
    clamped = [jnp.maximum(c, 0) for c in chunks]
    k = jnp.concatenate([k_ref[0, 0, c] for c in clamped], axis=0)
    vTs = [vT_ref[0, 0, c] for c in clamped]
    _, l, acc = _flash_step(k, vTs, qT_ref[0, 0, 0], bias, sink_ref[0], jnp.ones((1, W), F32),
                            jnp.zeros((HEAD_DIM, W), F32))
    o_ref[0] = _heads_to_rows(acc / l, H).astype(BF16)


def _swa_attn(qT, k, vT, sink_rows, seq):
    bsz, G, nq = qT.shape[:3]
    W = B_QPG * TQ
    nch = k.shape[2]
    per_bg5 = lambda b, g, i: (b, g, 0, 0, 0)
    return pl.pallas_call(
        _swa_attn_kernel,
        grid=(bsz, G, nq),
        in_specs=[pl.BlockSpec((1, 1, 1, HEAD_DIM, W), lambda b, g, i: (b, g, i, 0, 0)),
                  pl.BlockSpec((1, 1, nch, KC, HEAD_DIM), per_bg5),
                  pl.BlockSpec((1, 1, nch, HEAD_DIM, KC), per_bg5),
                  pl.BlockSpec((1, 1, W), lambda b, g, i: (g, 0, 0))],
        out_specs=pl.BlockSpec((1, TQ, B_QPG * HEAD_DIM), lambda b, g, i: (b, i, g)),
        out_shape=jax.ShapeDtypeStruct((bsz, seq, D_MODEL), BF16),
        compiler_params=_cparams("parallel", "parallel", "arbitrary"),
        name="swa_attention",
    )(qT, k, vT, sink_rows)


def _swa_proj_kernel(x_ref, gkv_ref, sckv_ref, shkv_ref, gq_ref, scq_ref, shq_ref,
                     wkv_ref, bkv_ref, wq_ref, bq_ref, rc_ref, ra_ref, rb_ref, k_ref, v_ref, q_ref):
    x = x_ref[...]
    rc, ra, rb = rc_ref[...], ra_ref[...], rb_ref[...]
    hkv = _norm_mod(x, gkv_ref[...], sckv_ref[0], shkv_ref[0]).astype(BF16)
    kv = _dot(hkv, wkv_ref[...]) + bkv_ref[...]
    kw = B_KV_HEADS * HEAD_DIM
    k_ref[...] = _rope(kv[:, :kw], rc, ra, rb).astype(BF16)
    v_ref[...] = kv[:, kw:].astype(BF16)
    hq = _norm_mod(x, gq_ref[...], scq_ref[0], shq_ref[0]).astype(BF16)
    q = _dot(hq, wq_ref[...]) + bq_ref[...]
    q_ref[...] = (_rope(q, rc, ra, rb) * QK_SCALE).astype(BF16)


def _swa_proj(x, gkv, sckv, shkv, gq, scq, shq, wkv, bkv, wq, bq, rc, ra, rb, seq):
    T = x.shape[0]
    per_b = seq // TM
    row = lambda i: (i, 0)
    fixed = lambda i: (0, 0)
    bat = lambda i: (i // per_b, 0, 0)
    kw = B_KV_HEADS * HEAD_DIM
    vec = pl.BlockSpec((1, D_MODEL), fixed)
    mod = pl.BlockSpec((1, 1, D_MODEL), bat)
    tab = pl.BlockSpec((TM, LANES), row)
    return pl.pallas_call(
        _swa_proj_kernel,
        grid=(T // TM,),
        in_specs=[pl.BlockSpec((TM, D_MODEL), row), vec, mod, mod, vec, mod, mod,
                  pl.BlockSpec(wkv.shape, fixed), pl.BlockSpec(bkv.shape, fixed),
                  pl.BlockSpec(wq.shape, fixed), pl.BlockSpec(bq.shape, fixed), tab, tab, tab],
        out_specs=[pl.BlockSpec((TM, kw), row), pl.BlockSpec((TM, kw), row),
                   pl.BlockSpec((TM, D_MODEL), row)],
        out_shape=[jax.ShapeDtypeStruct((T, kw), BF16), jax.ShapeDtypeStruct((T, kw), BF16),
                   jax.ShapeDtypeStruct((T, D_MODEL), BF16)],
        compiler_params=_cparams("parallel"),
        name="swa_proj",
    )(x, gkv, sckv, shkv, gq, scq, shq, wkv, bkv, wq, bq, rc, ra, rb)


def _oproj_kernel(x_ref, w_ref, b_ref, res_ref, gate_ref, o_ref):
    y = _dot(x_ref[...], w_ref[...]) + b_ref[...]
    o_ref[...] = res_ref[...] + gate_ref[0] * y


def _oproj(x, w, b, res, gate, seq):
    T = x.shape[0]
    per_b = seq // TM
    row = lambda i: (i, 0)
    fixed = lambda i: (0, 0)
    return pl.pallas_call(
        _oproj_kernel,
        grid=(T // TM,),
        in_specs=[pl.BlockSpec((TM, x.shape[1]), row), pl.BlockSpec(w.shape, fixed),
                  pl.BlockSpec((1, D_MODEL), fixed), pl.BlockSpec((TM, D_MODEL), row),
                  pl.BlockSpec((1, 1, D_MODEL), lambda i: (i // per_b, 0, 0))],
        out_specs=pl.BlockSpec((TM, D_MODEL), row),
        out_shape=jax.ShapeDtypeStruct((T, D_MODEL), F32),
        compiler_params=_cparams("parallel"),
        name="out_proj",
    )(x, w, b, res, gate)


def _router_kernel(x_ref, g_ref, sc_ref, sh_ref, wh_ref, wl_ref, b_ref, tri_ref, hn_ref, tw_ref, cnt_ref, run_ref):
    @pl.when(pl.program_id(0) == 0)
    def _():
        run_ref[...] = jnp.zeros(run_ref.shape, F32)

    hn = _norm_mod(x_ref[...], g_ref[...], sc_ref[0], sh_ref[0])
    hn_ref[...] = hn
    hi = hn.astype(BF16)
    lo = (hn - hi.astype(F32)).astype(BF16)
    wh = wh_ref[...]
    logits = _dot(hi, wh) + _dot(lo, wh) + _dot(hi, wl_ref[...]) + b_ref[...]
    col = lax.broadcasted_iota(I32, logits.shape, 1)
    cur = jnp.where(col < N_EXPERTS, logits, -jnp.inf)
    vals, idxs = [], []
    for _ in range(TOP_K):
        mx = jnp.max(cur, axis=-1, keepdims=True)
        ix = jnp.min(jnp.where(cur == mx, col, LANES), axis=-1, keepdims=True)
        vals.append(mx)
        idxs.append(ix)
        cur = jnp.where(col == ix, -jnp.inf, cur)
    es = [jnp.exp(v - vals[0]) for v in vals]
    den = es[0] + es[1] + es[2] + es[3]

    hots = [col == ix for ix in idxs]
    osum = (hots[0] | hots[1] | hots[2] | hots[3]).astype(F32)
    before = _dot(tri_ref[...], osum.astype(BF16)) + run_ref[...]
    ranks = [jnp.sum(jnp.where(hot, before, 0.0), axis=-1, keepdims=True) for hot in hots]
    run_ref[...] = run_ref[...] + jnp.sum(osum, axis=0, keepdims=True)
    cnt_ref[...] = run_ref[...]

    packed = jnp.zeros(logits.shape, F32)
    for k in range(TOP_K):
        packed = jnp.where(col == k, idxs[k].astype(F32), packed)
        packed = jnp.where(col == TOP_K + k, es[k] / den, packed)
        packed = jnp.where(col == 2 * TOP_K + k, ranks[k], packed)
    tw_ref[...] = packed.T[:4 * TOP_K, :]


def _router(x, g, scale, shift, wh, wl, b, seq):
    T = x.shape[0]
    per_b = seq // TM
    row = lambda i: (i, 0)
    fixed = lambda i: (0, 0)
    bat = lambda i: (i // per_b, 0, 0)
    tri = jnp.asarray(np.tril(np.ones((TM, TM), np.float32), -1), BF16)
    return pl.pallas_call(
        _router_kernel,
        grid=(T // TM,),
        in_specs=[pl.BlockSpec((TM, D_MODEL), row), pl.BlockSpec((1, D_MODEL), fixed),
                  pl.BlockSpec((1, 1, D_MODEL), bat), pl.BlockSpec((1, 1, D_MODEL), bat),
                  pl.BlockSpec(wh.shape, fixed), pl.BlockSpec(wl.shape, fixed),
                  pl.BlockSpec((1, LANES), fixed), pl.BlockSpec((TM, TM), fixed)],
        out_specs=[pl.BlockSpec((TM, D_MODEL), row), pl.BlockSpec((4 * TOP_K, TM), lambda i: (0, i)),
                   pl.BlockSpec((1, LANES), fixed)],
        out_shape=[jax.ShapeDtypeStruct((T, D_MODEL), F32),
                   jax.ShapeDtypeStruct((4 * TOP_K, T), F32),
                   jax.ShapeDtypeStruct((1, LANES), F32)],
        scratch_shapes=[pltpu.VMEM((1, LANES), F32)],
        compiler_params=_cparams("arbitrary"),
        name="moe_router",
    )(x, g, scale, shift, wh, wl, b, tri)


def _dispatch_kernel(ids_ref, hn_ref, xs_in_ref, xs_ref, sem):
    del xs_in_ref

    def row_copy(r, slot):
        return pltpu.make_async_copy(hn_ref.at[pl.ds(r, 1)], xs_ref.at[pl.ds(slot, 1)], sem)

    def issue(r, carry):
        for k in range(TOP_K):
            row_copy(r, ids_ref[0, 0, k * TD + r]).start()
        return carry

    lax.fori_loop(0, TD, issue, 0, unroll=8)

    def drain(r, carry):
        for k in range(TOP_K):
            row_copy(r, ids_ref[0, 0, k * TD + r]).wait()
        return carry

    lax.fori_loop(0, TD, drain, 0, unroll=8)


def _dispatch(hn, slot_tiles, xs_zero):
    T = hn.shape[0]
    return pl.pallas_call(
        _dispatch_kernel,
        grid=(T // TD,),
        in_specs=[pl.BlockSpec((1, 1, TOP_K * TD), lambda i: (i, 0, 0), memory_space=pltpu.SMEM),
                  pl.BlockSpec((TD, D_MODEL), lambda i: (i, 0)),
                  pl.BlockSpec(memory_space=pl.ANY)],
        out_specs=pl.BlockSpec(memory_space=pl.ANY),
        out_shape=jax.ShapeDtypeStruct(xs_zero.shape, xs_zero.dtype),
        input_output_aliases={2: 0},
        scratch_shapes=[pltpu.SemaphoreType.DMA(())],
        compiler_params=_cparams("arbitrary"),
        name="moe_dispatch",
    )(slot_tiles, hn, xs_zero)


def _experts_kernel(be_ref, nu_ref, x_ref, wgu_ref, bgu_ref, wd_ref, bd_ref, o_ref, wgu_s, wd_s):
    i = pl.program_id(0)
    used = i < nu_ref[0]
    new_expert = (i == 0) | (be_ref[i] != be_ref[jnp.maximum(i - 1, 0)])

    @pl.when(used & new_expert)
    def _():
        wgu_s[...] = wgu_ref[0].astype(BF16)
        wd_s[...] = wd_ref[0].astype(BF16)

    @pl.when(used)
    def _():
        x = x_ref[...].astype(BF16)
        gu = _dot(x, wgu_s[...]) + bgu_ref[0]
        glu = jnp.minimum(gu[:, :D_FF], SWIGLU_LIMIT)
        lin = jnp.clip(gu[:, D_FF:], -SWIGLU_LIMIT, SWIGLU_LIMIT)
        act = glu * jax.nn.sigmoid(SWIGLU_ALPHA * glu) * (lin + 1.0)
        o_ref[...] = _dot(act.astype(BF16), wd_s[...]) + bd_ref[0]

    @pl.when(i >= nu_ref[0])
    def _():
        o_ref[...] = jnp.zeros(o_ref.shape, o_ref.dtype)


def _experts(blk_expert, n_used, xs, wgu, bgu, wd, bd):
    n_slots = xs.shape[0]
    n_blk = n_slots // MOE_BLOCK
    row = lambda i, be, nu: (i, 0)
    ex3 = lambda i, be, nu: (be[i], 0, 0)
    grid_spec = pltpu.PrefetchScalarGridSpec(
        num_scalar_prefetch=2,
        grid=(n_blk,),
        in_specs=[pl.BlockSpec((MOE_BLOCK, D_MODEL), row),
                  pl.BlockSpec((1, D_MODEL, 2 * D_FF), ex3),
                  pl.BlockSpec((1, 1, 2 * D_FF), ex3),
                  pl.BlockSpec((1, D_FF, D_MODEL), ex3),
                  pl.BlockSpec((1, 1, D_MODEL), ex3)],
        out_specs=pl.BlockSpec((MOE_BLOCK, D_MODEL), row),
        scratch_shapes=[pltpu.VMEM((D_MODEL, 2 * D_FF), BF16), pltpu.VMEM((D_FF, D_MODEL), BF16)])
    return pl.pallas_call(
        _experts_kernel,
        grid_spec=grid_spec,
        out_shape=jax.ShapeDtypeStruct((n_slots, D_MODEL), F32),
        compiler_params=pltpu.CompilerParams(dimension_semantics=("arbitrary",),
                                             vmem_limit_bytes=EXPERTS_VMEM_LIMIT),
        name="moe_experts",
    )(blk_expert, n_used, xs, wgu, bgu.reshape(N_EXPERTS, 1, 2 * D_FF), wd, bd.reshape(N_EXPERTS, 1, D_MODEL))


def _combine_kernel(ids_ref, nxt_ref, outs_ref, w_ref, res_ref, gate_ref, fg_ref, out_ref, buf, sem, *, final_norm):
    i = pl.program_id(0)
    n = pl.num_programs(0)

    def row_copy(ids, r, k, par):
        return pltpu.make_async_copy(outs_ref.at[pl.ds(ids[0, 0, k * TC + r], 1)],
                                     buf.at[par, k, pl.ds(r, 1)], sem.at[par])

    def issue(ids, par):
        def body(r, carry):
            for k in range(TOP_K):
                row_copy(ids, r, k, par).start()
            return carry
        lax.fori_loop(0, TC, body, 0, unroll=8)

    @pl.when(i == 0)
    def _():
        issue(ids_ref, 0)

    @pl.when(i + 1 < n)
    def _():
        issue(nxt_ref, (i + 1) % 2)

    par = i % 2

    def drain(r, carry):
        for k in range(TOP_K):
            row_copy(ids_ref, r, k, par).wait()
        return carry

    lax.fori_loop(0, TC, drain, 0, unroll=8)

    w = w_ref[...]
    y = w[:, 0:1] * buf[par, 0]
    for k in range(1, TOP_K):
        y = y + w[:, k:k + 1] * buf[par, k]
    h = res_ref[...] + gate_ref[0] * y
    if final_norm:
        h = h * lax.rsqrt(jnp.mean(h * h, axis=-1, keepdims=True) + NORM_EPS) * fg_ref[...]
    out_ref[...] = h


def _combine(outs, slot_tiles, wts, res, gate, fg, seq, final_norm):
    T = res.shape[0]
    per_b = seq // TC
    nblk = T // TC
    row = lambda i: (i, 0)
    fixed = lambda i: (0, 0)
    ids = lambda i: (i, 0, 0)
    nxt = lambda i: (jnp.minimum(i + 1, nblk - 1), 0, 0)
    return pl.pallas_call(
        functools.partial(_combine_kernel, final_norm=final_norm),
        grid=(nblk,),
        in_specs=[pl.BlockSpec((1, 1, TOP_K * TC), ids, memory_space=pltpu.SMEM),
                  pl.BlockSpec((1, 1, TOP_K * TC), nxt, memory_space=pltpu.SMEM),
                  pl.BlockSpec(memory_space=pl.ANY),
                  pl.BlockSpec((TC, TOP_K), row), pl.BlockSpec((TC, D_MODEL), row),
                  pl.BlockSpec((1, 1, D_MODEL), lambda i: (i // per_b, 0, 0)),
                  pl.BlockSpec((1, D_MODEL), fixed)],
        out_specs=pl.BlockSpec((TC, D_MODEL), row),
        out_shape=jax.ShapeDtypeStruct((T, D_MODEL), F32),
        scratch_shapes=[pltpu.VMEM((2, TOP_K, TC, D_MODEL), F32), pltpu.SemaphoreType.DMA((2,))],
        compiler_params=_cparams("arbitrary"),
        name="moe_combine",
    )(slot_tiles, slot_tiles, outs, wts, res, gate, fg)


def _rope_tables(positions):
    inv = ROPE_THETA ** (-jnp.arange(0, 2 * ROT_HALF, 2, dtype=F32) / (2 * ROT_HALF))
    ang = positions.astype(F32)[..., None] * inv
    cos, sin = jnp.cos(ang), jnp.sin(ang)
    lead = cos.shape[:-1]
    ones = jnp.ones(lead + (HEAD_DIM - 2 * ROT_HALF,), F32)
    zeros = jnp.zeros(lead + (HEAD_DIM - 2 * ROT_HALF,), F32)
    z8 = jnp.zeros(lead + (ROT_HALF,), F32)
    rc = jnp.concatenate([cos, cos, ones], -1)
    ra = jnp.concatenate([-sin, z8, zeros], -1)
    rb = jnp.concatenate([z8, sin, zeros], -1)
    two = lambda t: jnp.concatenate([t, t], -1)
    return two(rc), two(ra), two(rb)


def _moe_slots(expert, rank, counts, n_tok):
    pad_counts = (counts + MOE_BLOCK - 1) // MOE_BLOCK * MOE_BLOCK
    pad_end = jnp.cumsum(pad_counts)
    pad_start = pad_end - pad_counts
    hot = expert[..., None] == jnp.arange(N_EXPERTS, dtype=I32)
    slot = jnp.sum(jnp.where(hot, pad_start.astype(I32), 0), axis=-1) + rank
    n_blk = (n_tok * TOP_K + N_EXPERTS * MOE_BLOCK) // MOE_BLOCK
    blk_start = jnp.arange(n_blk) * MOE_BLOCK
    blk_expert = jnp.minimum(jnp.sum(pad_end[None, :] <= blk_start[:, None], axis=1), N_EXPERTS - 1).astype(I32)
    n_used = (pad_end[-1] // MOE_BLOCK).astype(I32).reshape(1)
    return slot.astype(I32), blk_expert, n_used


def _slot_tiles(slot, tile):
    n_tok = slot.shape[1]
    return slot.reshape(TOP_K, n_tok // tile, tile).transpose(1, 0, 2).reshape(n_tok // tile, 1, TOP_K * tile)


def _moe_layer(h, g, scale, shift, gate, router_w, router_b, wgu, bgu, wd, bd, fg, seq, final_norm):
    T = h.shape[0]
    rw = jnp.pad(router_w, ((0, 0), (0, LANES - N_EXPERTS)))
    rwh = rw.astype(BF16)
    rwl = (rw - rwh.astype(F32)).astype(BF16)
    rb = jnp.pad(router_b, (0, LANES - N_EXPERTS)).reshape(1, LANES)
    hn, tw, cnt = _router(h, g, scale, shift, rwh, rwl, rb, seq)
    expert = tw[:TOP_K].astype(I32)
    wts = tw[TOP_K:2 * TOP_K].T
    rank = tw[2 * TOP_K:3 * TOP_K].astype(I32)
    slot, blk_expert, n_used = _moe_slots(expert, rank, cnt[0, :N_EXPERTS].astype(I32), T)
    n_slots = T * TOP_K + N_EXPERTS * MOE_BLOCK
    xs = _dispatch(hn, _slot_tiles(slot, TD), jnp.zeros((n_slots, D_MODEL), F32))
    outs = _experts(blk_expert, n_used, xs, wgu, bgu, wd, bd)
    return _combine(outs, _slot_tiles(slot, TC), wts, h, gate, fg, seq, final_norm)


def _cover_matrix(seq):
    n_cmp = (seq - CMP_LEN) // CMP_STRIDE + 1
    n_sel = seq // SEL_LEN
    cmp_lo = np.arange(n_cmp) * CMP_STRIDE
    sel_lo = np.arange(n_sel) * SEL_LEN
    cover = ((cmp_lo[:, None] < sel_lo[None, :] + SEL_LEN)
             & (sel_lo[None, :] < cmp_lo[:, None] + CMP_LEN)).astype(np.float32)
    covT = np.zeros((n_sel, seq // CMP_STRIDE), np.float32)
    covT[:, :n_cmp] = cover.T
    return jnp.asarray(covT, BF16)


def kernel(x, c, positions, mod_w, mod_b, norm_g, final_norm_g, a_w_in, a_cmp_pe, a_cmp_w1, a_cmp_w2,
           a_w_out, kv_norm_g, kv_mod_w, kv_mod_b, kv_w, kv_b, b_w_q, b_b_q, b_sinks, b_w_out, b_b_out,
           router_w, router_b, w_gate_up, b_gate_up, w_down, b_down):
    bsz, seq, _ = x.shape
    T = bsz * seq
    G = A_GROUPS
    nq = seq // TQ
    nch = seq // KC
    hd = A_GROUPS * A_QPG * HEAD_DIM
    kd = A_GROUPS * HEAD_DIM

    mods = _mods(c, mod_w.reshape(4, D_MODEL, 3 * D_MODEL), mod_b.reshape(4, 3 * D_MODEL))
    kv_mods = _mods(c, kv_mod_w[None], kv_mod_b[None])[0]

    def split_mod(m, n):
        return [p.reshape(bsz, 1, D_MODEL) for p in jnp.split(m, n, axis=-1)]

    rc, ra, rb = _rope_tables(positions)
    rc2, ra2, rb2 = (t.reshape(T, LANES) for t in (rc, ra, rb))
    h = x.reshape(T, D_MODEL)
    vec = lambda v: v.reshape(1, D_MODEL)

    shift, scale, gate = split_mod(mods[0], 3)
    w_in = a_w_in[0]
    wq = w_in[:, :hd].astype(BF16)
    wkv = w_in[:, hd:hd + 6 * kd].astype(BF16)
    wg = jnp.pad(w_in[:, hd + 6 * kd:], ((0, 0), (0, LANES - 3 * A_GROUPS * A_QPG))).astype(BF16)
    q, kv, gates = _nsa_proj(h, vec(norm_g[0, 0]), scale, shift, wq, wkv, wg, rc2, ra2, rb2, seq)

    qT = q.reshape(bsz, nq, TQ, G, A_QPG, HEAD_DIM).transpose(0, 3, 1, 5, 4, 2).reshape(
        bsz, G, nq, HEAD_DIM, A_QPG * TQ)
    kv6 = kv.reshape(bsz, seq, 6, G, HEAD_DIM)
    craw = kv6[:, :, 0:2].transpose(2, 0, 3, 1, 4).reshape(2, bsz, G, seq // CMP_STRIDE, CMP_STRIDE * HEAD_DIM)
    kmaj = lambda a, ch: a.transpose(0, 2, 1, 3).reshape(bsz, G, seq // ch, ch, HEAD_DIM)
    vmaj = lambda a, ch: a.reshape(bsz, seq // ch, ch, G, HEAD_DIM).transpose(0, 3, 1, 4, 2)
    ks, vsT = kmaj(kv6[:, :, 2], KS), vmaj(kv6[:, :, 3], KS)
    kw, vwT = kmaj(kv6[:, :, 4], KC), vmaj(kv6[:, :, 5], KC)
    gT = gates[:, :3 * G * A_QPG].reshape(bsz, nq, TQ, G, A_QPG, 3).transpose(0, 3, 1, 5, 4, 2).reshape(
        bsz, G, nq, 3, A_QPG * TQ)
    gT = jnp.pad(gT, ((0, 0), (0, 0), (0, 0), (0, 5), (0, 0)))

    ncmp = seq // CMP_STRIDE
    end_pos = jnp.minimum(jnp.arange(ncmp) * CMP_STRIDE + CMP_LEN - 1, seq - 1)
    crc, cra, crb = _rope_tables(positions[:, end_pos])
    ident = (jnp.ones_like(crc), jnp.zeros_like(cra), jnp.zeros_like(crb))
    ctab = [jnp.stack([t, i]) for t, i in zip((crc, cra, crb), ident)]
    pe = a_cmp_pe[0].reshape(2, 2, CMP_STRIDE * HEAD_DIM)
    w1 = a_cmp_w1[0].reshape(2, 2, CMP_STRIDE * HEAD_DIM, CMP_HID).astype(BF16)
    w2 = jnp.pad(a_cmp_w2[0], ((0, 0), (0, 0), (0, LANES - HEAD_DIM))).astype(BF16)
    kvc = _compress(craw, pe, w1, w2, *ctab)
    kc = kvc[0]
    vcT = kvc[1].transpose(0, 1, 3, 2)

    o = _nsa_attn(qT, kc, vcT, ks, vsT, kw, vwT, gT, _cover_matrix(seq), seq)
    h = _oproj(o.reshape(T, D_MODEL), a_w_out[0].astype(BF16), jnp.zeros((1, D_MODEL), F32), h, gate, seq)

    shift, scale, gate = split_mod(mods[1], 3)
    h = _moe_layer(h, vec(norm_g[0, 1]), scale, shift, gate, router_w[0], router_b[0], w_gate_up[0],
                   b_gate_up[0], w_down[0], b_down[0], vec(final_norm_g), seq, False)

    kv_shift, kv_scale = split_mod(kv_mods, 2)
    shift, scale, gate = split_mod(mods[2], 3)
    k, v, q = _swa_proj(h, vec(kv_norm_g), kv_scale, kv_shift, vec(norm_g[1, 0]), scale, shift,
                        kv_w.astype(BF16), kv_b.reshape(1, -1), b_w_q[0].astype(BF16), b_b_q[0].reshape(1, -1),
                        rc2, ra2, rb2, seq)
    Gb = B_KV_HEADS
    qT = q.reshape(bsz, nq, TQ, Gb, B_QPG, HEAD_DIM).transpose(0, 3, 1, 5, 4, 2).reshape(
        bsz, Gb, nq, HEAD_DIM, B_QPG * TQ)
    kb = k.reshape(bsz, seq, Gb, HEAD_DIM).transpose(0, 2, 1, 3).reshape(bsz, Gb, nch, KC, HEAD_DIM)
    vbT = v.reshape(bsz, nch, KC, Gb, HEAD_DIM).transpose(0, 3, 1, 4, 2)
    sink_rows = jnp.repeat(b_sinks[0].reshape(Gb, 1, B_QPG), TQ, axis=-1)
    o = _swa_attn(qT, kb, vbT, sink_rows, seq)
    h = _oproj(o.reshape(T, D_MODEL), b_w_out[0].astype(BF16), b_b_out[0].reshape(1, D_MODEL), h, gate, seq)

    shift, scale, gate = split_mod(mods[3], 3)
    out = _moe_layer(h, vec(norm_g[1, 1]), scale, shift, gate, router_w[1], router_b[1], w_gate_up[1],
                     b_gate_up[1], w_down[1], b_down[1], vec(final_norm_g), seq, True)
    return out.reshape(bsz, seq, D_MODEL)
```

```python
import functools

import numpy as np
import jax
import jax.numpy as jnp
from jax import lax
from jax.experimental import pallas as pl
from jax.experimental.pallas import tpu as pltpu

F32 = jnp.float32
BF16 = jnp.bfloat16
I32 = jnp.int32

D_MODEL = 1024
HEAD_DIM = 64
ROT_HALF = 8
ROPE_THETA = 500000.0
NORM_EPS = 1e-5
QK_SCALE = HEAD_DIM ** -0.5
A_GROUPS = 4
A_QPG = 4
CMP_LEN = 32
CMP_STRIDE = 16
CMP_HID = 128
SEL_LEN = 64
SEL_TOPN = 16
A_WINDOW = 512
B_KV_HEADS = 2
B_QPG = 8
B_WINDOW = 128
N_EXPERTS = 32
TOP_K = 4
D_FF = 1024
SWIGLU_LIMIT = 7.0
SWIGLU_ALPHA = 1.702
MOE_BLOCK = 256

LANES = 128
TQ = 256
KC = 128
KS = 512
CHAIN_LANES = 1024
GATE_ROWS = 16
TM = 512
TD = 256
TC = 256
MASKED = -1e30
M_INIT = -1e29
VMEM_LIMIT = 48 * 1024 * 1024
EXPERTS_VMEM_LIMIT = 56 * 1024 * 1024


def _cparams(*sem):
    return pltpu.CompilerParams(dimension_semantics=sem, vmem_limit_bytes=VMEM_LIMIT)


def _dot(a, b):
    return jnp.dot(a, b, preferred_element_type=F32)


def _norm_mod(x, g, scale, shift):
    y = x * lax.rsqrt(jnp.mean(x * x, axis=-1, keepdims=True) + NORM_EPS)
    return (y * g) * (1.0 + scale) + shift


def _rope128(v, rc, ra, rb):
    return v * rc + pltpu.roll(v, LANES - ROT_HALF, 1) * ra + pltpu.roll(v, ROT_HALF, 1) * rb


def _rope(v, rc, ra, rb):
    parts = [_rope128(v[:, s * LANES:(s + 1) * LANES], rc, ra, rb) for s in range(v.shape[1] // LANES)]
    return parts[0] if len(parts) == 1 else jnp.concatenate(parts, axis=1)


def _mods_kernel(c_ref, w_ref, b_ref, o_ref):
    c = c_ref[...]
    ca = c * jax.nn.sigmoid(c)
    o_ref[0] = jnp.dot(ca, w_ref[0], preferred_element_type=F32,
                       precision=lax.Precision.HIGHEST) + b_ref[0]


def _mods(c, w, b):
    n, _, N = w.shape
    bsz = c.shape[0]
    tn = 1024
    return pl.pallas_call(
        _mods_kernel,
        grid=(n, N // tn),
        in_specs=[pl.BlockSpec((bsz, D_MODEL), lambda i, j: (0, 0)),
                  pl.BlockSpec((1, D_MODEL, tn), lambda i, j: (i, 0, j)),
                  pl.BlockSpec((1, 1, tn), lambda i, j: (i, 0, j))],
        out_specs=pl.BlockSpec((1, bsz, tn), lambda i, j: (i, 0, j)),
        out_shape=jax.ShapeDtypeStruct((n, bsz, N), F32),
        compiler_params=_cparams("parallel", "parallel"),
        name="adaln_mods",
    )(c, w, b.reshape(n, 1, N))


def _store_qT(qT_ref, q, n_groups, n_heads):
    qTf = q.T
    for g in range(n_groups):
        for j in range(TM // TQ):
            pieces = [qTf[(g * n_heads + h) * HEAD_DIM:(g * n_heads + h + 1) * HEAD_DIM, j * TQ:(j + 1) * TQ]
                      for h in range(n_heads)]
            qT_ref[0, g, j] = jnp.concatenate(pieces, axis=1).astype(BF16)


def _store_k(k_ref, k, n_groups, chunk):
    for g in range(n_groups):
        for c in range(TM // chunk):
            k_ref[0, g, c] = k[c * chunk:(c + 1) * chunk, g * HEAD_DIM:(g + 1) * HEAD_DIM].astype(BF16)


def _store_vT(vT_ref, v, n_groups, chunk):
    vTf = v.T
    for g in range(n_groups):
        for c in range(TM // chunk):
            vT_ref[0, g, c] = vTf[g * HEAD_DIM:(g + 1) * HEAD_DIM, c * chunk:(c + 1) * chunk].astype(BF16)


def _nsa_proj_kernel(x_ref, g_ref, sc_ref, sh_ref, wq_ref, wkv_ref, wg_ref, rc_ref, ra_ref, rb_ref,
                     qT_ref, craw_ref, ks_ref, vsT_ref, kw_ref, vwT_ref, gT_ref):
    hn = _norm_mod(x_ref[...], g_ref[...], sc_ref[0], sh_ref[0]).astype(BF16)
    rc, ra, rb = rc_ref[...], ra_ref[...], rb_ref[...]
    q = _dot(hn, wq_ref[...])
    _store_qT(qT_ref, _rope(q, rc, ra, rb) * QK_SCALE, A_GROUPS, A_QPG)
    kv = _dot(hn, wkv_ref[...])
    kd = A_GROUPS * HEAD_DIM
    craw_ref[...] = kv[:, :2 * kd].astype(BF16)
    _store_k(ks_ref, _rope(kv[:, 2 * kd:3 * kd], rc, ra, rb), A_GROUPS, KS)
    _store_vT(vsT_ref, kv[:, 3 * kd:4 * kd], A_GROUPS, KS)
    _store_k(kw_ref, _rope(kv[:, 4 * kd:5 * kd], rc, ra, rb), A_GROUPS, KC)
    _store_vT(vwT_ref, kv[:, 5 * kd:6 * kd], A_GROUPS, KC)
    gates = jax.nn.sigmoid(_dot(hn, wg_ref[...]))
    gT_ref[0] = gates.T[:A_GROUPS * GATE_ROWS, :]


def _nsa_proj(x, g, scale, shift, wq, wkv, wg, rc, ra, rb, seq):
    T = x.shape[0]
    bsz = T // seq
    per_b = seq // TM
    G = A_GROUPS
    kd = G * HEAD_DIM
    row = lambda i: (i, 0)
    fixed = lambda i: (0, 0)
    bat = lambda i: (i // per_b, 0, 0)
    tile5 = lambda i: (i // per_b, 0, i % per_b, 0, 0)
    return pl.pallas_call(
        _nsa_proj_kernel,
        grid=(T // TM,),
        in_specs=[pl.BlockSpec((TM, D_MODEL), row),
                  pl.BlockSpec((1, D_MODEL), fixed),
                  pl.BlockSpec((1, 1, D_MODEL), bat),
                  pl.BlockSpec((1, 1, D_MODEL), bat),
                  pl.BlockSpec(wq.shape, fixed),
                  pl.BlockSpec(wkv.shape, fixed),
                  pl.BlockSpec(wg.shape, fixed),
                  pl.BlockSpec((TM, LANES), row),
                  pl.BlockSpec((TM, LANES), row),
                  pl.BlockSpec((TM, LANES), row)],
        out_specs=[pl.BlockSpec((1, G, TM // TQ, HEAD_DIM, A_QPG * TQ), tile5),
                   pl.BlockSpec((TM, 2 * kd), row),
                   pl.BlockSpec((1, G, TM // KS, KS, HEAD_DIM), tile5),
                   pl.BlockSpec((1, G, TM // KS, HEAD_DIM, KS), tile5),
                   pl.BlockSpec((1, G, TM // KC, KC, HEAD_DIM), tile5),
                   pl.BlockSpec((1, G, TM // KC, HEAD_DIM, KC), tile5),
                   pl.BlockSpec((1, G * GATE_ROWS, TM), lambda i: (i // per_b, 0, i % per_b))],
        out_shape=[jax.ShapeDtypeStruct((bsz, G, seq // TQ, HEAD_DIM, A_QPG * TQ), BF16),
                   jax.ShapeDtypeStruct((T, 2 * kd), BF16),
                   jax.ShapeDtypeStruct((bsz, G, seq // KS, KS, HEAD_DIM), BF16),
                   jax.ShapeDtypeStruct((bsz, G, seq // KS, HEAD_DIM, KS), BF16),
                   jax.ShapeDtypeStruct((bsz, G, seq // KC, KC, HEAD_DIM), BF16),
                   jax.ShapeDtypeStruct((bsz, G, seq // KC, HEAD_DIM, KC), BF16),
                   jax.ShapeDtypeStruct((bsz, G * GATE_ROWS, seq), F32)],
        compiler_params=_cparams("parallel"),
        name="nsa_proj",
    )(x, g, scale, shift, wq, wkv, wg, rc, ra, rb)


def _compress_kernel(c_ref, pe_ref, w1_ref, w2_ref, rc_ref, ra_ref, rb_ref, o_ref):
    c = c_ref[0, 0, 0].astype(F32)
    a = _dot((c + pe_ref[0, 0:1, :]).astype(BF16), w1_ref[0, 0])
    b = _dot((c + pe_ref[0, 1:2, :]).astype(BF16), w1_ref[0, 1])
    z = a + pltpu.roll(b, b.shape[0] - 1, 0)
    hid = jax.nn.gelu(z)
    out = _dot(hid.astype(BF16), w2_ref[0])
    out = _rope128(out, rc_ref[0, 0], ra_ref[0, 0], rb_ref[0, 0])
    o_ref[0, 0, 0] = out[:, :HEAD_DIM].astype(BF16)


def _compress(craw, pe, w1, w2, rc, ra, rb):
    _, bsz, G, nch, width = craw.shape
    return pl.pallas_call(
        _compress_kernel,
        grid=(2, bsz, G),
        in_specs=[pl.BlockSpec((1, 1, 1, nch, width), lambda s, b, g: (s, b, g, 0, 0)),
                  pl.BlockSpec((1, 2, width), lambda s, b, g: (s, 0, 0)),
                  pl.BlockSpec((1, 2, width, CMP_HID), lambda s, b, g: (s, 0, 0, 0)),
                  pl.BlockSpec((1, CMP_HID, LANES), lambda s, b, g: (s, 0, 0)),
                  pl.BlockSpec((1, 1, nch, LANES), lambda s, b, g: (s, b, 0, 0)),
                  pl.BlockSpec((1, 1, nch, LANES), lambda s, b, g: (s, b, 0, 0)),
                  pl.BlockSpec((1, 1, nch, LANES), lambda s, b, g: (s, b, 0, 0))],
        out_specs=pl.BlockSpec((1, 1, 1, nch, HEAD_DIM), lambda s, b, g: (s, b, g, 0, 0)),
        out_shape=jax.ShapeDtypeStruct((2, bsz, G, nch, HEAD_DIM), BF16),
        compiler_params=_cparams("parallel", "parallel", "parallel"),
        name="nsa_compress",
    )(craw, pe, w1, w2, rc, ra, rb)


def _flash_step(k, vTs, qT, bias, m, l, acc):
    outs = []
    for g0 in range(0, qT.shape[1], CHAIN_LANES):
        lanes = slice(g0, g0 + CHAIN_LANES)
        s = _dot(k, qT[:, lanes]) + bias[:, lanes]
        m_new = jnp.maximum(m[:, lanes], jnp.max(s, axis=0, keepdims=True))
        alpha = jnp.exp(m[:, lanes] - m_new)
        p = jnp.exp(s - m_new)
        l_new = alpha * l[:, lanes] + jnp.sum(p, axis=0, keepdims=True)
        acc_new = alpha * acc[:, lanes]
        row = 0
        for vT in vTs:
            n = vT.shape[1]
            acc_new = acc_new + _dot(vT, p[row:row + n].astype(BF16))
            row += n
        outs.append((m_new, l_new, acc_new))
    if len(outs) == 1:
        return outs[0]
    return tuple(jnp.concatenate([o[j] for o in outs], axis=1) for j in range(3))


def _tile_lanes(x, n):
    return x if n == 1 else jnp.concatenate([x] * n, axis=1)


def _heads_to_rows(oT, n_heads):
    stacked = jnp.concatenate([oT[:, h * TQ:(h + 1) * TQ] for h in range(n_heads)], axis=0)
    return stacked.T


def _nsa_attn_kernel(qT_ref, kc_ref, vcT_ref, ks_ref, vsT_ref, kw_ref, vwT_ref, gT_ref, covT_ref,
                     o_ref, selb_ref):
    qi = pl.program_id(2)
    H = A_QPG
    W = H * TQ
    qT = qT_ref[0, 0, 0]
    qpos = qi * TQ + lax.broadcasted_iota(I32, (1, TQ), 1)
    ncmp = kc_ref.shape[2]

    n_io = lax.broadcasted_iota(I32, (ncmp, 1), 0)
    end_pos = n_io * CMP_STRIDE + (CMP_LEN - 1)
    valid = (end_pos <= qpos) & (n_io < ncmp - 1)
    bias_c = _tile_lanes(jnp.where(valid, 0.0, MASKED), H)
    s = _dot(kc_ref[0, 0], qT) + bias_c
    m = jnp.maximum(jnp.max(s, axis=0, keepdims=True), M_INIT)
    p = jnp.exp(s - m)
    den = jnp.maximum(jnp.sum(p, axis=0, keepdims=True), jnp.finfo(F32).tiny)
    pn = p / den
    o_c = _dot(vcT_ref[0, 0], pn.astype(BF16))

    psum = pn[:, 0:TQ]
    for h in range(1, H):
        psum = psum + pn[:, h * TQ:(h + 1) * TQ]
    p_hi = psum.astype(BF16)
    p_lo = (psum - p_hi.astype(F32)).astype(BF16)
    covT = covT_ref[...]
    impT = _dot(covT, p_hi) + _dot(covT, p_lo)

    n_sel = covT.shape[0]
    j_io = lax.broadcasted_iota(I32, (n_sel, 1), 0)
    cur = qpos // SEL_LEN
    forced = (j_io == 0) | (j_io == cur) | (j_io == cur - 1)
    causal = j_io <= cur
    score = jnp.where(forced, jnp.inf, jnp.where(causal, impT, -jnp.inf))
    rank = jnp.zeros((n_sel, TQ), I32)
    for i in range(n_sel):
        vi = score[i:i + 1, :]
        ahead = (vi > score) | ((vi == score) & (i < j_io))
        rank = rank + ahead.astype(I32)
    selb = jnp.where(rank < SEL_TOPN, 0.0, MASKED)
    for j in range(n_sel):
        selb_ref[j] = selb[j:j + 1, :]

    init = (jnp.full((1, W), M_INIT, F32), jnp.zeros((1, W), F32), jnp.zeros((HEAD_DIM, W), F32))

    k_io = lax.broadcasted_iota(I32, (KC, 1), 0)
    win_chunks = [qi * (TQ // KC) - A_WINDOW // KC + j for j in range((A_WINDOW + TQ) // KC)]
    biases = []
    for c in win_chunks:
        kpos = c * KC + k_io
        biases.append(jnp.where((kpos <= qpos) & (kpos > qpos - A_WINDOW) & (kpos >= 0), 0.0, MASKED))
    clamped = [jnp.maximum(c, 0) for c in win_chunks]
    k_w = jnp.concatenate([kw_ref[0, 0, c] for c in clamped], axis=0)
    bias_w = _tile_lanes(jnp.concatenate(biases, axis=0), H)
    _, l_w, acc_w = _flash_step(k_w, [vwT_ref[0, 0, c] for c in clamped], qT, bias_w, *init)

    ks_io = lax.broadcasted_iota(I32, (KS, 1), 0)
    per_step = KS // SEL_LEN

    def sel_body(st, carry):
        kpos = st * KS + ks_io
        blk = jnp.concatenate(
            [jnp.broadcast_to(selb_ref[per_step * st + r], (SEL_LEN, TQ)) for r in range(per_step)], axis=0)
        bias = _tile_lanes(jnp.where(kpos <= qpos, blk, MASKED), H)
        return _flash_step(ks_ref[0, 0, st], [vsT_ref[0, 0, st]], qT, bias, *carry)

    _, l_s, acc_s = lax.fori_loop(0, (qi * TQ + TQ - 1) // KS + 1, sel_body, init)

    g16 = gT_ref[0]

    def gate(branch):
        return jnp.concatenate([g16[branch * H + h:branch * H + h + 1, :] for h in range(H)], axis=1)

    oT = gate(0) * o_c + gate(1) * (acc_s / l_s) + gate(2) * (acc_w / l_w)
    o_ref[0] = _heads_to_rows(oT, H).astype(BF16)


def _nsa_attn(qT, kc, vcT, ks, vsT, kw, vwT, gT, covT, seq):
    bsz, G, nq = qT.shape[:3]
    W = A_QPG * TQ
    ncmp = kc.shape[2]
    n_sel = covT.shape[0]
    per_bg = lambda b, g, i: (b, g, 0, 0)
    per_bg5 = lambda b, g, i: (b, g, 0, 0, 0)
    return pl.pallas_call(
        _nsa_attn_kernel,
        grid=(bsz, G, nq),
        in_specs=[pl.BlockSpec((1, 1, 1, HEAD_DIM, W), lambda b, g, i: (b, g, i, 0, 0)),
                  pl.BlockSpec((1, 1, ncmp, HEAD_DIM), per_bg),
                  pl.BlockSpec((1, 1, HEAD_DIM, ncmp), per_bg),
                  pl.BlockSpec((1, 1) + ks.shape[2:], per_bg5),
                  pl.BlockSpec((1, 1) + vsT.shape[2:], per_bg5),
                  pl.BlockSpec((1, 1) + kw.shape[2:], per_bg5),
                  pl.BlockSpec((1, 1) + vwT.shape[2:], per_bg5),
                  pl.BlockSpec((1, GATE_ROWS, TQ), lambda b, g, i: (b, g, i)),
                  pl.BlockSpec((n_sel, ncmp), lambda b, g, i: (0, 0))],
        out_specs=pl.BlockSpec((1, TQ, A_QPG * HEAD_DIM), lambda b, g, i: (b, i, g)),
        out_shape=jax.ShapeDtypeStruct((bsz, seq, D_MODEL), BF16),
        scratch_shapes=[pltpu.VMEM((n_sel, 1, TQ), F32)],
        compiler_params=_cparams("parallel", "parallel", "arbitrary"),
        name="nsa_attention",
    )(qT, kc, vcT, ks, vsT, kw, vwT, gT, covT)


def _swa_attn_kernel(qT_ref, k_ref, vT_ref, sink_ref, o_ref):
    qi = pl.program_id(2)
    H = B_QPG
    W = H * TQ
    qpos = qi * TQ + lax.broadcasted_iota(I32, (1, TQ), 1)
    k_io = lax.broadcasted_iota(I32, (KC, 1), 0)
    chunks = [qi * (TQ // KC) - B_WINDOW // KC + j for j in range((B_WINDOW + TQ) // KC)]
    biases = []
    for c in chunks:
        kpos = c * KC + k_io
        biases.append(jnp.where((kpos <= qpos) & (kpos > qpos - B_WINDOW) & (kpos >= 0), 0.0, MASKED))
    bias = _tile_lanes(jnp.concatenate(biases, axis=0), H)
    clamped = [jnp.maximum(c, 0) for c in chunks]
    k = jnp.concatenate([k_ref[0, 0, c] for c in clamped], axis=0)
    vTs = [vT_ref[0, 0, c] for c in clamped]
    _, l, acc = _flash_step(k, vTs, qT_ref[0, 0, 0], bias, sink_ref[0], jnp.ones((1, W), F32),
                            jnp.zeros((HEAD_DIM, W), F32))
    o_ref[0] = _heads_to_rows(acc / l, H).astype(BF16)


def _swa_attn(qT, k, vT, sink_rows, seq):
    bsz, G, nq = qT.shape[:3]
    W = B_QPG * TQ
    nch = k.shape[2]
    per_bg5 = lambda b, g, i: (b, g, 0, 0, 0)
    return pl.pallas_call(
        _swa_attn_kernel,
        grid=(bsz, G, nq),
        in_specs=[pl.BlockSpec((1, 1, 1, HEAD_DIM, W), lambda b, g, i: (b, g, i, 0, 0)),
                  pl.BlockSpec((1, 1, nch, KC, HEAD_DIM), per_bg5),
                  pl.BlockSpec((1, 1, nch, HEAD_DIM, KC), per_bg5),
                  pl.BlockSpec((1, 1, W), lambda b, g, i: (g, 0, 0))],
        out_specs=pl.BlockSpec((1, TQ, B_QPG * HEAD_DIM), lambda b, g, i: (b, i, g)),
        out_shape=jax.ShapeDtypeStruct((bsz, seq, D_MODEL), BF16),
        compiler_params=_cparams("parallel", "parallel", "arbitrary"),
        name="swa_attention",
    )(qT, k, vT, sink_rows)


def _swa_proj_kernel(x_ref, gkv_ref, sckv_ref, shkv_ref, gq_ref, scq_ref, shq_ref,
                     wkv_ref, bkv_ref, wq_ref, bq_ref, rc_ref, ra_ref, rb_ref, k_ref, vT_ref, qT_ref):
    x = x_ref[...]
    rc, ra, rb = rc_ref[...], ra_ref[...], rb_ref[...]
    hkv = _norm_mod(x, gkv_ref[...], sckv_ref[0], shkv_ref[0]).astype(BF16)
    kv = _dot(hkv, wkv_ref[...]) + bkv_ref[...]
    kw = B_KV_HEADS * HEAD_DIM
    _store_k(k_ref, _rope(kv[:, :kw], rc, ra, rb), B_KV_HEADS, KC)
    _store_vT(vT_ref, kv[:, kw:], B_KV_HEADS, KC)
    hq = _norm_mod(x, gq_ref[...], scq_ref[0], shq_ref[0]).astype(BF16)
    q = _dot(hq, wq_ref[...]) + bq_ref[...]
    _store_qT(qT_ref, _rope(q, rc, ra, rb) * QK_SCALE, B_KV_HEADS, B_QPG)


def _swa_proj(x, gkv, sckv, shkv, gq, scq, shq, wkv, bkv, wq, bq, rc, ra, rb, seq):
    T = x.shape[0]
    bsz = T // seq
    per_b = seq // TM
    G = B_KV_HEADS
    row = lambda i: (i, 0)
    fixed = lambda i: (0, 0)
    bat = lambda i: (i // per_b, 0, 0)
    tile5 = lambda i: (i // per_b, 0, i % per_b, 0, 0)
    vec = pl.BlockSpec((1, D_MODEL), fixed)
    mod = pl.BlockSpec((1, 1, D_MODEL), bat)
    tab = pl.BlockSpec((TM, LANES), row)
    return pl.pallas_call(
        _swa_proj_kernel,
        grid=(T // TM,),
        in_specs=[pl.BlockSpec((TM, D_MODEL), row), vec, mod, mod, vec, mod, mod,
                  pl.BlockSpec(wkv.shape, fixed), pl.BlockSpec(bkv.shape, fixed),
                  pl.BlockSpec(wq.shape, fixed), pl.BlockSpec(bq.shape, fixed), tab, tab, tab],
        out_specs=[pl.BlockSpec((1, G, TM // KC, KC, HEAD_DIM), tile5),
                   pl.BlockSpec((1, G, TM // KC, HEAD_DIM, KC), tile5),
                   pl.BlockSpec((1, G, TM // TQ, HEAD_DIM, B_QPG * TQ), tile5)],
        out_shape=[jax.ShapeDtypeStruct((bsz, G, seq // KC, KC, HEAD_DIM), BF16),
                   jax.ShapeDtypeStruct((bsz, G, seq // KC, HEAD_DIM, KC), BF16),
                   jax.ShapeDtypeStruct((bsz, G, seq // TQ, HEAD_DIM, B_QPG * TQ), BF16)],
        compiler_params=_cparams("parallel"),
        name="swa_proj",
    )(x, gkv, sckv, shkv, gq, scq, shq, wkv, bkv, wq, bq, rc, ra, rb)


def _oproj_kernel(x_ref, w_ref, b_ref, res_ref, gate_ref, o_ref):
    y = _dot(x_ref[...], w_ref[...]) + b_ref[...]
    o_ref[...] = res_ref[...] + gate_ref[0] * y


def _oproj(x, w, b, res, gate, seq):
    T = x.shape[0]
    per_b = seq // TM
    row = lambda i: (i, 0)
    fixed = lambda i: (0, 0)
    return pl.pallas_call(
        _oproj_kernel,
        grid=(T // TM,),
        in_specs=[pl.BlockSpec((TM, x.shape[1]), row), pl.BlockSpec(w.shape, fixed),
                  pl.BlockSpec((1, D_MODEL), fixed), pl.BlockSpec((TM, D_MODEL), row),
                  pl.BlockSpec((1, 1, D_MODEL), lambda i: (i // per_b, 0, 0))],
        out_specs=pl.BlockSpec((TM, D_MODEL), row),
        out_shape=jax.ShapeDtypeStruct((T, D_MODEL), F32),
        compiler_params=_cparams("parallel"),
        name="out_proj",
    )(x, w, b, res, gate)


def _router_kernel(x_ref, g_ref, sc_ref, sh_ref, wh_ref, wl_ref, b_ref, tri_ref, hn_ref, tw_ref, cnt_ref, run_ref):
    @pl.when(pl.program_id(0) == 0)
    def _():
        run_ref[...] = jnp.zeros(run_ref.shape, F32)

    hn = _norm_mod(x_ref[...], g_ref[...], sc_ref[0], sh_ref[0])
    hn_ref[...] = hn
    hi = hn.astype(BF16)
    lo = (hn - hi.astype(F32)).astype(BF16)
    wh = wh_ref[...]
    logits = _dot(hi, wh) + _dot(lo, wh) + _dot(hi, wl_ref[...]) + b_ref[...]
    col = lax.broadcasted_iota(I32, logits.shape, 1)
    cur = jnp.where(col < N_EXPERTS, logits, -jnp.inf)
    vals, idxs = [], []
    for _ in range(TOP_K):
        mx = jnp.max(cur, axis=-1, keepdims=True)
        ix = jnp.min(jnp.where(cur == mx, col, LANES), axis=-1, keepdims=True)
        vals.append(mx)
        idxs.append(ix)
        cur = jnp.where(col == ix, -jnp.inf, cur)
    es = [jnp.exp(v - vals[0]) for v in vals]
    den = es[0] + es[1] + es[2] + es[3]

    hots = [col == ix for ix in idxs]
    osum = (hots[0] | hots[1] | hots[2] | hots[3]).astype(F32)
    before = _dot(tri_ref[...], osum.astype(BF16)) + run_ref[...]
    ranks = [jnp.sum(jnp.where(hot, before, 0.0), axis=-1, keepdims=True) for hot in hots]
    run_ref[...] = run_ref[...] + jnp.sum(osum, axis=0, keepdims=True)
    cnt_ref[...] = run_ref[...]

    packed = jnp.zeros(logits.shape, F32)
    for k in range(TOP_K):
        packed = jnp.where(col == k, idxs[k].astype(F32), packed)
        packed = jnp.where(col == TOP_K + k, es[k] / den, packed)
        packed = jnp.where(col == 2 * TOP_K + k, ranks[k], packed)
    tw_ref[...] = packed.T[:4 * TOP_K, :]


def _router(x, g, scale, shift, wh, wl, b, seq):
    T = x.shape[0]
    per_b = seq // TM
    row = lambda i: (i, 0)
    fixed = lambda i: (0, 0)
    bat = lambda i: (i // per_b, 0, 0)
    tri = jnp.asarray(np.tril(np.ones((TM, TM), np.float32), -1), BF16)
    return pl.pallas_call(
        _router_kernel,
        grid=(T // TM,),
        in_specs=[pl.BlockSpec((TM, D_MODEL), row), pl.BlockSpec((1, D_MODEL), fixed),
                  pl.BlockSpec((1, 1, D_MODEL), bat), pl.BlockSpec((1, 1, D_MODEL), bat),
                  pl.BlockSpec(wh.shape, fixed), pl.BlockSpec(wl.shape, fixed),
                  pl.BlockSpec((1, LANES), fixed), pl.BlockSpec((TM, TM), fixed)],
        out_specs=[pl.BlockSpec((TM, D_MODEL), row), pl.BlockSpec((4 * TOP_K, TM), lambda i: (0, i)),
                   pl.BlockSpec((1, LANES), fixed)],
        out_shape=[jax.ShapeDtypeStruct((T, D_MODEL), F32),
                   jax.ShapeDtypeStruct((4 * TOP_K, T), F32),
                   jax.ShapeDtypeStruct((1, LANES), F32)],
        scratch_shapes=[pltpu.VMEM((1, LANES), F32)],
        compiler_params=_cparams("arbitrary"),
        name="moe_router",
    )(x, g, scale, shift, wh, wl, b, tri)


def _dispatch_kernel(ids_ref, hn_ref, xs_in_ref, xs_ref, sem):
    del xs_in_ref

    def row_copy(r, slot):
        return pltpu.make_async_copy(hn_ref.at[pl.ds(r, 1)], xs_ref.at[pl.ds(slot, 1)], sem)

    def issue(r, carry):
        for k in range(TOP_K):
            row_copy(r, ids_ref[0, 0, k * TD + r]).start()
        return carry

    lax.fori_loop(0, TD, issue, 0, unroll=8)

    def drain(r, carry):
        for k in range(TOP_K):
            row_copy(r, ids_ref[0, 0, k * TD + r]).wait()
        return carry

    lax.fori_loop(0, TD, drain, 0, unroll=8)


def _dispatch(hn, slot_tiles, xs_zero):
    T = hn.shape[0]
    return pl.pallas_call(
        _dispatch_kernel,
        grid=(T // TD,),
        in_specs=[pl.BlockSpec((1, 1, TOP_K * TD), lambda i: (i, 0, 0), memory_space=pltpu.SMEM),
                  pl.BlockSpec((TD, D_MODEL), lambda i: (i, 0)),
                  pl.BlockSpec(memory_space=pl.ANY)],
        out_specs=pl.BlockSpec(memory_space=pl.ANY),
        out_shape=jax.ShapeDtypeStruct(xs_zero.shape, xs_zero.dtype),
        input_output_aliases={2: 0},
        scratch_shapes=[pltpu.SemaphoreType.DMA(())],
        compiler_params=_cparams("arbitrary"),
        name="moe_dispatch",
    )(slot_tiles, hn, xs_zero)


def _experts_kernel(be_ref, nu_ref, x_ref, wgu_ref, bgu_ref, wd_ref, bd_ref, o_ref, wgu_s, wd_s):
    i = pl.program_id(0)
    used = i < nu_ref[0]
    new_expert = (i == 0) | (be_ref[i] != be_ref[jnp.maximum(i - 1, 0)])

    @pl.when(used & new_expert)
    def _():
        wgu_s[...] = wgu_ref[0].astype(BF16)
        wd_s[...] = wd_ref[0].astype(BF16)

    @pl.when(used)
    def _():
        x = x_ref[...].astype(BF16)
        gu = _dot(x, wgu_s[...]) + bgu_ref[0]
        glu = jnp.minimum(gu[:, :D_FF], SWIGLU_LIMIT)
        lin = jnp.clip(gu[:, D_FF:], -SWIGLU_LIMIT, SWIGLU_LIMIT)
        act = glu * jax.nn.sigmoid(SWIGLU_ALPHA * glu) * (lin + 1.0)
        o_ref[...] = _dot(act.astype(BF16), wd_s[...]) + bd_ref[0]

    @pl.when(i >= nu_ref[0])
    def _():
        o_ref[...] = jnp.zeros(o_ref.shape, o_ref.dtype)


def _experts(blk_expert, n_used, xs, wgu, bgu, wd, bd, layer):
    n_slots = xs.shape[0]
    n_blk = n_slots // MOE_BLOCK
    row = lambda i, be, nu: (i, 0)
    ex3 = lambda i, be, nu: (layer * N_EXPERTS + be[i], 0, 0)
    grid_spec = pltpu.PrefetchScalarGridSpec(
        num_scalar_prefetch=2,
        grid=(n_blk,),
        in_specs=[pl.BlockSpec((MOE_BLOCK, D_MODEL), row),
                  pl.BlockSpec((1, D_MODEL, 2 * D_FF), ex3),
                  pl.BlockSpec((1, 1, 2 * D_FF), ex3),
                  pl.BlockSpec((1, D_FF, D_MODEL), ex3),
                  pl.BlockSpec((1, 1, D_MODEL), ex3)],
        out_specs=pl.BlockSpec((MOE_BLOCK, D_MODEL), row),
        scratch_shapes=[pltpu.VMEM((D_MODEL, 2 * D_FF), BF16), pltpu.VMEM((D_FF, D_MODEL), BF16)])
    return pl.pallas_call(
        _experts_kernel,
        grid_spec=grid_spec,
        out_shape=jax.ShapeDtypeStruct((n_slots, D_MODEL), F32),
        compiler_params=pltpu.CompilerParams(dimension_semantics=("arbitrary",),
                                             vmem_limit_bytes=EXPERTS_VMEM_LIMIT),
        name="moe_experts",
    )(blk_expert, n_used, xs, wgu.reshape(-1, D_MODEL, 2 * D_FF), bgu.reshape(-1, 1, 2 * D_FF),
      wd.reshape(-1, D_FF, D_MODEL), bd.reshape(-1, 1, D_MODEL))


def _combine_kernel(ids_ref, nxt_ref, outs_ref, w_ref, res_ref, gate_ref, fg_ref, out_ref, buf, sem, *, final_norm):
    i = pl.program_id(0)
    n = pl.num_programs(0)

    def row_copy(ids, r, k, par):
        return pltpu.make_async_copy(outs_ref.at[pl.ds(ids[0, 0, k * TC + r], 1)],
                                     buf.at[par, k, pl.ds(r, 1)], sem.at[par])

    def issue(ids, par):
        def body(r, carry):
            for k in range(TOP_K):
                row_copy(ids, r, k, par).start()
            return carry
        lax.fori_loop(0, TC, body, 0, unroll=8)

    @pl.when(i == 0)
    def _():
        issue(ids_ref, 0)

    @pl.when(i + 1 < n)
    def _():
        issue(nxt_ref, (i + 1) % 2)

    par = i % 2

    def drain(r, carry):
        for k in range(TOP_K):
            row_copy(ids_ref, r, k, par).wait()
        return carry

    lax.fori_loop(0, TC, drain, 0, unroll=8)

    w = w_ref[...]
    y = w[:, 0:1] * buf[par, 0]
    for k in range(1, TOP_K):
        y = y + w[:, k:k + 1] * buf[par, k]
    h = res_ref[...] + gate_ref[0] * y
    if final_norm:
        h = h * lax.rsqrt(jnp.mean(h * h, axis=-1, keepdims=True) + NORM_EPS) * fg_ref[...]
    out_ref[...] = h


def _combine(outs, slot_tiles, wts, res, gate, fg, seq, final_norm):
    T = res.shape[0]
    per_b = seq // TC
    nblk = T // TC
    row = lambda i: (i, 0)
    fixed = lambda i: (0, 0)
    ids = lambda i: (i, 0, 0)
    nxt = lambda i: (jnp.minimum(i + 1, nblk - 1), 0, 0)
    return pl.pallas_call(
        functools.partial(_combine_kernel, final_norm=final_norm),
        grid=(nblk,),
        in_specs=[pl.BlockSpec((1, 1, TOP_K * TC), ids, memory_space=pltpu.SMEM),
                  pl.BlockSpec((1, 1, TOP_K * TC), nxt, memory_space=pltpu.SMEM),
                  pl.BlockSpec(memory_space=pl.ANY),
                  pl.BlockSpec((TC, TOP_K), row), pl.BlockSpec((TC, D_MODEL), row),
                  pl.BlockSpec((1, 1, D_MODEL), lambda i: (i // per_b, 0, 0)),
                  pl.BlockSpec((1, D_MODEL), fixed)],
        out_specs=pl.BlockSpec((TC, D_MODEL), row),
        out_shape=jax.ShapeDtypeStruct((T, D_MODEL), F32),
        scratch_shapes=[pltpu.VMEM((2, TOP_K, TC, D_MODEL), F32), pltpu.SemaphoreType.DMA((2,))],
        compiler_params=_cparams("arbitrary"),
        name="moe_combine",
    )(slot_tiles, slot_tiles, outs, wts, res, gate, fg)


def _rope_tables(positions):
    inv = ROPE_THETA ** (-jnp.arange(0, 2 * ROT_HALF, 2, dtype=F32) / (2 * ROT_HALF))
    ang = positions.astype(F32)[..., None] * inv
    cos, sin = jnp.cos(ang), jnp.sin(ang)
    lead = cos.shape[:-1]
    ones = jnp.ones(lead + (HEAD_DIM - 2 * ROT_HALF,), F32)
    zeros = jnp.zeros(lead + (HEAD_DIM - 2 * ROT_HALF,), F32)
    z8 = jnp.zeros(lead + (ROT_HALF,), F32)
    rc = jnp.concatenate([cos, cos, ones], -1)
    ra = jnp.concatenate([-sin, z8, zeros], -1)
    rb = jnp.concatenate([z8, sin, zeros], -1)
    two = lambda t: jnp.concatenate([t, t], -1)
    return two(rc), two(ra), two(rb)


def _moe_slots(expert, rank, counts, n_tok):
    pad_counts = (counts + MOE_BLOCK - 1) // MOE_BLOCK * MOE_BLOCK
    pad_end = jnp.cumsum(pad_counts)
    pad_start = pad_end - pad_counts
    hot = expert[..., None] == jnp.arange(N_EXPERTS, dtype=I32)
    slot = jnp.sum(jnp.where(hot, pad_start.astype(I32), 0), axis=-1) + rank
    n_blk = (n_tok * TOP_K + N_EXPERTS * MOE_BLOCK) // MOE_BLOCK
    blk_start = jnp.arange(n_blk) * MOE_BLOCK
    blk_expert = jnp.minimum(jnp.sum(pad_end[None, :] <= blk_start[:, None], axis=1), N_EXPERTS - 1).astype(I32)
    n_used = (pad_end[-1] // MOE_BLOCK).astype(I32).reshape(1)
    return slot.astype(I32), blk_expert, n_used


def _slot_tiles(slot, tile):
    n_tok = slot.shape[1]
    return slot.reshape(TOP_K, n_tok // tile, tile).transpose(1, 0, 2).reshape(n_tok // tile, 1, TOP_K * tile)


def _moe_layer(h, g, scale, shift, gate, router_w, router_b, wgu, bgu, wd, bd, fg, seq, layer, final_norm):
    T = h.shape[0]
    rw = jnp.pad(router_w, ((0, 0), (0, LANES - N_EXPERTS)))
    rwh = rw.astype(BF16)
    rwl = (rw - rwh.astype(F32)).astype(BF16)
    rb = jnp.pad(router_b, (0, LANES - N_EXPERTS)).reshape(1, LANES)
    hn, tw, cnt = _router(h, g, scale, shift, rwh, rwl, rb, seq)
    expert = tw[:TOP_K].astype(I32)
    wts = tw[TOP_K:2 * TOP_K].T
    rank = tw[2 * TOP_K:3 * TOP_K].astype(I32)
    slot, blk_expert, n_used = _moe_slots(expert, rank, cnt[0, :N_EXPERTS].astype(I32), T)
    n_slots = T * TOP_K + N_EXPERTS * MOE_BLOCK
    xs = _dispatch(hn, _slot_tiles(slot, TD), jnp.zeros((n_slots, D_MODEL), F32))
    outs = _experts(blk_expert, n_used, xs, wgu, bgu, wd, bd, layer)
    return _combine(outs, _slot_tiles(slot, TC), wts, h, gate, fg, seq, final_norm)


def _cover_matrix(seq):
    n_cmp = (seq - CMP_LEN) // CMP_STRIDE + 1
    n_sel = seq // SEL_LEN
    cmp_lo = np.arange(n_cmp) * CMP_STRIDE
    sel_lo = np.arange(n_sel) * SEL_LEN
    cover = ((cmp_lo[:, None] < sel_lo[None, :] + SEL_LEN)
             & (sel_lo[None, :] < cmp_lo[:, None] + CMP_LEN)).astype(np.float32)
    covT = np.zeros((n_sel, seq // CMP_STRIDE), np.float32)
    covT[:, :n_cmp] = cover.T
    return jnp.asarray(covT, BF16)


def kernel(x, c, positions, mod_w, mod_b, norm_g, final_norm_g, a_w_in, a_cmp_pe, a_cmp_w1, a_cmp_w2,
           a_w_out, kv_norm_g, kv_mod_w, kv_mod_b, kv_w, kv_b, b_w_q, b_b_q, b_sinks, b_w_out, b_b_out,
           router_w, router_b, w_gate_up, b_gate_up, w_down, b_down):
    bsz, seq, _ = x.shape
    T = bsz * seq
    G = A_GROUPS
    hd = A_GROUPS * A_QPG * HEAD_DIM
    kd = A_GROUPS * HEAD_DIM

    mods = _mods(c, mod_w.reshape(4, D_MODEL, 3 * D_MODEL), mod_b.reshape(4, 3 * D_MODEL))
    kv_mods = _mods(c, kv_mod_w[None], kv_mod_b[None])[0]

    def split_mod(m, n):
        return [p.reshape(bsz, 1, D_MODEL) for p in jnp.split(m, n, axis=-1)]

    rc, ra, rb = _rope_tables(positions)
    rc2, ra2, rb2 = (t.reshape(T, LANES) for t in (rc, ra, rb))
    h = x.reshape(T, D_MODEL)
    vec = lambda v: v.reshape(1, D_MODEL)

    shift, scale, gate = split_mod(mods[0], 3)
    w_in = a_w_in[0]
    wq = w_in[:, :hd].astype(BF16)
    wkv = w_in[:, hd:hd + 6 * kd].astype(BF16)
    ggrp, ghead, gbr = np.meshgrid(np.arange(G), np.arange(A_QPG), np.arange(3), indexing="ij")
    gate_cols = (ggrp * GATE_ROWS + gbr * A_QPG + ghead).reshape(-1)
    wg = jnp.zeros((D_MODEL, LANES), F32).at[:, gate_cols].set(w_in[:, hd + 6 * kd:]).astype(BF16)
    qT, ckv, ks, vsT, kw, vwT, gT = _nsa_proj(h, vec(norm_g[0, 0]), scale, shift, wq, wkv, wg,
                                              rc2, ra2, rb2, seq)
    craw = ckv.reshape(bsz, seq, 2, G, HEAD_DIM).transpose(2, 0, 3, 1, 4).reshape(
        2, bsz, G, seq // CMP_STRIDE, CMP_STRIDE * HEAD_DIM)

    ncmp = seq // CMP_STRIDE
    end_pos = jnp.minimum(jnp.arange(ncmp) * CMP_STRIDE + CMP_LEN - 1, seq - 1)
    crc, cra, crb = _rope_tables(positions[:, end_pos])
    ident = (jnp.ones_like(crc), jnp.zeros_like(cra), jnp.zeros_like(crb))
    ctab = [jnp.stack([t, i]) for t, i in zip((crc, cra, crb), ident)]
    pe = a_cmp_pe[0].reshape(2, 2, CMP_STRIDE * HEAD_DIM)
    w1 = a_cmp_w1[0].reshape(2, 2, CMP_STRIDE * HEAD_DIM, CMP_HID).astype(BF16)
    w2 = jnp.pad(a_cmp_w2[0], ((0, 0), (0, 0), (0, LANES - HEAD_DIM))).astype(BF16)
    kvc = _compress(craw, pe, w1, w2, *ctab)
    kc = kvc[0]
    vcT = kvc[1].transpose(0, 1, 3, 2)

    o = _nsa_attn(qT, kc, vcT, ks, vsT, kw, vwT, gT, _cover_matrix(seq), seq)
    h = _oproj(o.reshape(T, D_MODEL), a_w_out[0].astype(BF16), jnp.zeros((1, D_MODEL), F32), h, gate, seq)

    shift, scale, gate = split_mod(mods[1], 3)
    h = _moe_layer(h, vec(norm_g[0, 1]), scale, shift, gate, router_w[0], router_b[0], w_gate_up,
                   b_gate_up, w_down, b_down, vec(final_norm_g), seq, 0, False)

    kv_shift, kv_scale = split_mod(kv_mods, 2)
    shift, scale, gate = split_mod(mods[2], 3)
    kb, vbT, qT = _swa_proj(h, vec(kv_norm_g), kv_scale, kv_shift, vec(norm_g[1, 0]), scale, shift,
                            kv_w.astype(BF16), kv_b.reshape(1, -1), b_w_q[0].astype(BF16),
                            b_b_q[0].reshape(1, -1), rc2, ra2, rb2, seq)
    sink_rows = jnp.repeat(b_sinks[0].reshape(B_KV_HEADS, 1, B_QPG), TQ, axis=-1)
    o = _swa_attn(qT, kb, vbT, sink_rows, seq)
    h = _oproj(o.reshape(T, D_MODEL), b_w_out[0].astype(BF16), b_b_out[0].reshape(1, D_MODEL), h, gate, seq)

    shift, scale, gate = split_mod(mods[3], 3)
    out = _moe_layer(h, vec(norm_g[1, 1]), scale, shift, gate, router_w[1], router_b[1], w_gate_up,
                     b_gate_up, w_down, b_down, vec(final_norm_g), seq, 1, True)
    return out.reshape(bsz, seq, D_MODEL)
```

```python
import functools

import numpy as np
import jax
import jax.numpy as jnp
from jax import lax
from jax.experimental import pallas as pl
from jax.experimental.pallas import tpu as pltpu

F32 = jnp.float32
BF16 = jnp.bfloat16
I32 = jnp.int32

D_MODEL = 1024
HEAD_DIM = 64
ROT_HALF = 8
ROPE_THETA = 500000.0
NORM_EPS = 1e-5
QK_SCALE = HEAD_DIM ** -0.5
A_GROUPS = 4
A_QPG = 4
CMP_LEN = 32
CMP_STRIDE = 16
CMP_HID = 128
SEL_LEN = 64
SEL_TOPN = 16
A_WINDOW = 512
B_KV_HEADS = 2
B_QPG = 8
B_WINDOW = 128
N_EXPERTS = 32
TOP_K = 4
D_FF = 1024
SWIGLU_LIMIT = 7.0
SWIGLU_ALPHA = 1.702
MOE_BLOCK = 256

LANES = 128
TQ = 256
KC = 128
KS = 512
CHAIN_LANES = 1024
GATE_ROWS = 16
ROW_TILE = D_MODEL // LANES
TM = 512
TD = 256
TC = 256
MASKED = -1e30
M_INIT = -1e29
VMEM_LIMIT = 48 * 1024 * 1024
EXPERTS_VMEM_LIMIT = 56 * 1024 * 1024


def _cparams(*sem):
    return pltpu.CompilerParams(dimension_semantics=sem, vmem_limit_bytes=VMEM_LIMIT)


def _dot(a, b):
    return jnp.dot(a, b, preferred_element_type=F32)


def _norm_mod(x, g, scale, shift):
    y = x * lax.rsqrt(jnp.mean(x * x, axis=-1, keepdims=True) + NORM_EPS)
    return (y * g) * (1.0 + scale) + shift


def _store_row_tiles(ref, x):
    for c in range(ROW_TILE):
        ref[pl.ds(c, x.shape[0], stride=ROW_TILE), :] = x[:, c * LANES:(c + 1) * LANES]


def _load_row_tiles(ref, rows):
    return jnp.concatenate([ref[pl.ds(c, rows, stride=ROW_TILE), :] for c in range(ROW_TILE)], axis=1)


def _rope128(v, rc, ra, rb):
    return v * rc + pltpu.roll(v, LANES - ROT_HALF, 1) * ra + pltpu.roll(v, ROT_HALF, 1) * rb


def _rope(v, rc, ra, rb):
    parts = [_rope128(v[:, s * LANES:(s + 1) * LANES], rc, ra, rb) for s in range(v.shape[1] // LANES)]
    return parts[0] if len(parts) == 1 else jnp.concatenate(parts, axis=1)


def _mods_kernel(c_ref, w_ref, b_ref, o_ref):
    c = c_ref[...]
    ca = c * jax.nn.sigmoid(c)
    o_ref[0] = jnp.dot(ca, w_ref[0], preferred_element_type=F32,
                       precision=lax.Precision.HIGHEST) + b_ref[0]


def _mods(c, w, b):
    n, _, N = w.shape
    bsz = c.shape[0]
    tn = 1024
    return pl.pallas_call(
        _mods_kernel,
        grid=(n, N // tn),
        in_specs=[pl.BlockSpec((bsz, D_MODEL), lambda i, j: (0, 0)),
                  pl.BlockSpec((1, D_MODEL, tn), lambda i, j: (i, 0, j)),
                  pl.BlockSpec((1, 1, tn), lambda i, j: (i, 0, j))],
        out_specs=pl.BlockSpec((1, bsz, tn), lambda i, j: (i, 0, j)),
        out_shape=jax.ShapeDtypeStruct((n, bsz, N), F32),
        compiler_params=_cparams("parallel", "parallel"),
        name="adaln_mods",
    )(c, w, b.reshape(n, 1, N))


def _store_qT(qT_ref, q, n_groups, n_heads):
    qTf = q.T
    for g in range(n_groups):
        for j in range(TM // TQ):
            pieces = [qTf[(g * n_heads + h) * HEAD_DIM:(g * n_heads + h + 1) * HEAD_DIM, j * TQ:(j + 1) * TQ]
                      for h in range(n_heads)]
            qT_ref[0, g, j] = jnp.concatenate(pieces, axis=1).astype(BF16)


def _store_k(k_ref, k, n_groups, chunk):
    for g in range(n_groups):
        for c in range(TM // chunk):
            k_ref[0, g, c] = k[c * chunk:(c + 1) * chunk, g * HEAD_DIM:(g + 1) * HEAD_DIM].astype(BF16)


def _store_vT(vT_ref, v, n_groups, chunk):
    vTf = v.T
    for g in range(n_groups):
        for c in range(TM // chunk):
            vT_ref[0, g, c] = vTf[g * HEAD_DIM:(g + 1) * HEAD_DIM, c * chunk:(c + 1) * chunk].astype(BF16)


def _nsa_proj_kernel(x_ref, g_ref, sc_ref, sh_ref, wq_ref, wkv_ref, wg_ref, rc_ref, ra_ref, rb_ref,
                     qT_ref, craw_ref, ks_ref, vsT_ref, kw_ref, vwT_ref, gT_ref):
    hn = _norm_mod(x_ref[...], g_ref[...], sc_ref[0], sh_ref[0]).astype(BF16)
    rc, ra, rb = rc_ref[...], ra_ref[...], rb_ref[...]
    q = _dot(hn, wq_ref[...])
    _store_qT(qT_ref, _rope(q, rc, ra, rb) * QK_SCALE, A_GROUPS, A_QPG)
    kv = _dot(hn, wkv_ref[...])
    kd = A_GROUPS * HEAD_DIM
    craw_ref[...] = kv[:, :2 * kd].astype(BF16)
    _store_k(ks_ref, _rope(kv[:, 2 * kd:3 * kd], rc, ra, rb), A_GROUPS, KS)
    _store_vT(vsT_ref, kv[:, 3 * kd:4 * kd], A_GROUPS, KS)
    _store_k(kw_ref, _rope(kv[:, 4 * kd:5 * kd], rc, ra, rb), A_GROUPS, KC)
    _store_vT(vwT_ref, kv[:, 5 * kd:6 * kd], A_GROUPS, KC)
    gates = jax.nn.sigmoid(_dot(hn, wg_ref[...]))
    gT_ref[0] = gates.T[:A_GROUPS * GATE_ROWS, :]


def _nsa_proj(x, g, scale, shift, wq, wkv, wg, rc, ra, rb, seq):
    T = x.shape[0]
    bsz = T // seq
    per_b = seq // TM
    G = A_GROUPS
    kd = G * HEAD_DIM
    row = lambda i: (i, 0)
    fixed = lambda i: (0, 0)
    bat = lambda i: (i // per_b, 0, 0)
    tile5 = lambda i: (i // per_b, 0, i % per_b, 0, 0)
    return pl.pallas_call(
        _nsa_proj_kernel,
        grid=(T // TM,),
        in_specs=[pl.BlockSpec((TM, D_MODEL), row),
                  pl.BlockSpec((1, D_MODEL), fixed),
                  pl.BlockSpec((1, 1, D_MODEL), bat),
                  pl.BlockSpec((1, 1, D_MODEL), bat),
                  pl.BlockSpec(wq.shape, fixed),
                  pl.BlockSpec(wkv.shape, fixed),
                  pl.BlockSpec(wg.shape, fixed),
                  pl.BlockSpec((TM, LANES), row),
                  pl.BlockSpec((TM, LANES), row),
                  pl.BlockSpec((TM, LANES), row)],
        out_specs=[pl.BlockSpec((1, G, TM // TQ, HEAD_DIM, A_QPG * TQ), tile5),
                   pl.BlockSpec((TM, 2 * kd), row),
                   pl.BlockSpec((1, G, TM // KS, KS, HEAD_DIM), tile5),
                   pl.BlockSpec((1, G, TM // KS, HEAD_DIM, KS), tile5),
                   pl.BlockSpec((1, G, TM // KC, KC, HEAD_DIM), tile5),
                   pl.BlockSpec((1, G, TM // KC, HEAD_DIM, KC), tile5),
                   pl.BlockSpec((1, G * GATE_ROWS, TM), lambda i: (i // per_b, 0, i % per_b))],
        out_shape=[jax.ShapeDtypeStruct((bsz, G, seq // TQ, HEAD_DIM, A_QPG * TQ), BF16),
                   jax.ShapeDtypeStruct((T, 2 * kd), BF16),
                   jax.ShapeDtypeStruct((bsz, G, seq // KS, KS, HEAD_DIM), BF16),
                   jax.ShapeDtypeStruct((bsz, G, seq // KS, HEAD_DIM, KS), BF16),
                   jax.ShapeDtypeStruct((bsz, G, seq // KC, KC, HEAD_DIM), BF16),
                   jax.ShapeDtypeStruct((bsz, G, seq // KC, HEAD_DIM, KC), BF16),
                   jax.ShapeDtypeStruct((bsz, G * GATE_ROWS, seq), F32)],
        compiler_params=_cparams("parallel"),
        name="nsa_proj",
    )(x, g, scale, shift, wq, wkv, wg, rc, ra, rb)


def _compress_kernel(c_ref, pe_ref, w1_ref, w2_ref, rc_ref, ra_ref, rb_ref, o_ref):
    c = c_ref[0, 0, 0].astype(F32)
    a = _dot((c + pe_ref[0, 0:1, :]).astype(BF16), w1_ref[0, 0])
    b = _dot((c + pe_ref[0, 1:2, :]).astype(BF16), w1_ref[0, 1])
    z = a + pltpu.roll(b, b.shape[0] - 1, 0)
    hid = jax.nn.gelu(z)
    out = _dot(hid.astype(BF16), w2_ref[0])
    out = _rope128(out, rc_ref[0, 0], ra_ref[0, 0], rb_ref[0, 0])
    o_ref[0, 0, 0] = out[:, :HEAD_DIM].astype(BF16)


def _compress(craw, pe, w1, w2, rc, ra, rb):
    _, bsz, G, nch, width = craw.shape
    return pl.pallas_call(
        _compress_kernel,
        grid=(2, bsz, G),
        in_specs=[pl.BlockSpec((1, 1, 1, nch, width), lambda s, b, g: (s, b, g, 0, 0)),
                  pl.BlockSpec((1, 2, width), lambda s, b, g: (s, 0, 0)),
                  pl.BlockSpec((1, 2, width, CMP_HID), lambda s, b, g: (s, 0, 0, 0)),
                  pl.BlockSpec((1, CMP_HID, LANES), lambda s, b, g: (s, 0, 0)),
                  pl.BlockSpec((1, 1, nch, LANES), lambda s, b, g: (s, b, 0, 0)),
                  pl.BlockSpec((1, 1, nch, LANES), lambda s, b, g: (s, b, 0, 0)),
                  pl.BlockSpec((1, 1, nch, LANES), lambda s, b, g: (s, b, 0, 0))],
        out_specs=pl.BlockSpec((1, 1, 1, nch, HEAD_DIM), lambda s, b, g: (s, b, g, 0, 0)),
        out_shape=jax.ShapeDtypeStruct((2, bsz, G, nch, HEAD_DIM), BF16),
        compiler_params=_cparams("parallel", "parallel", "parallel"),
        name="nsa_compress",
    )(craw, pe, w1, w2, rc, ra, rb)


def _flash_step(k, vTs, qT, bias, m, l, acc):
    outs = []
    for g0 in range(0, qT.shape[1], CHAIN_LANES):
        lanes = slice(g0, g0 + CHAIN_LANES)
        s = _dot(k, qT[:, lanes]) + bias[:, lanes]
        m_new = jnp.maximum(m[:, lanes], jnp.max(s, axis=0, keepdims=True))
        alpha = jnp.exp(m[:, lanes] - m_new)
        p = jnp.exp(s - m_new)
        l_new = alpha * l[:, lanes] + jnp.sum(p, axis=0, keepdims=True)
        acc_new = alpha * acc[:, lanes]
        row = 0
        for vT in vTs:
            n = vT.shape[1]
            acc_new = acc_new + _dot(vT, p[row:row + n].astype(BF16))
            row += n
        outs.append((m_new, l_new, acc_new))
    if len(outs) == 1:
        return outs[0]
    return tuple(jnp.concatenate([o[j] for o in outs], axis=1) for j in range(3))


def _tile_lanes(x, n):
    return x if n == 1 else jnp.concatenate([x] * n, axis=1)


def _heads_to_rows(oT, n_heads):
    stacked = jnp.concatenate([oT[:, h * TQ:(h + 1) * TQ] for h in range(n_heads)], axis=0)
    return stacked.T


def _nsa_attn_kernel(qT_ref, kc_ref, vcT_ref, ks_ref, vsT_ref, kw_ref, vwT_ref, gT_ref, covT_ref,
                     o_ref, selb_ref):
    qi = pl.program_id(2)
    H = A_QPG
    W = H * TQ
    qT = qT_ref[0, 0, 0]
    qpos = qi * TQ + lax.broadcasted_iota(I32, (1, TQ), 1)
    ncmp = kc_ref.shape[2]

    n_io = lax.broadcasted_iota(I32, (ncmp, 1), 0)
    end_pos = n_io * CMP_STRIDE + (CMP_LEN - 1)
    valid = (end_pos <= qpos) & (n_io < ncmp - 1)
    bias_c = _tile_lanes(jnp.where(valid, 0.0, MASKED), H)
    s = _dot(kc_ref[0, 0], qT) + bias_c
    m = jnp.maximum(jnp.max(s, axis=0, keepdims=True), M_INIT)
    p = jnp.exp(s - m)
    den = jnp.maximum(jnp.sum(p, axis=0, keepdims=True), jnp.finfo(F32).tiny)
    pn = p / den
    o_c = _dot(vcT_ref[0, 0], pn.astype(BF16))

    psum = pn[:, 0:TQ]
    for h in range(1, H):
        psum = psum + pn[:, h * TQ:(h + 1) * TQ]
    p_hi = psum.astype(BF16)
    p_lo = (psum - p_hi.astype(F32)).astype(BF16)
    covT = covT_ref[...]
    impT = _dot(covT, p_hi) + _dot(covT, p_lo)

    n_sel = covT.shape[0]
    j_io = lax.broadcasted_iota(I32, (n_sel, 1), 0)
    cur = qpos // SEL_LEN
    forced = (j_io == 0) | (j_io == cur) | (j_io == cur - 1)
    causal = j_io <= cur
    score = jnp.where(forced, jnp.inf, jnp.where(causal, impT, -jnp.inf))
    rank = jnp.zeros((n_sel, TQ), I32)
    for i in range(n_sel):
        vi = score[i:i + 1, :]
        ahead = (vi > score) | ((vi == score) & (i < j_io))
        rank = rank + ahead.astype(I32)
    selb = jnp.where(rank < SEL_TOPN, 0.0, MASKED)
    for j in range(n_sel):
        selb_ref[j] = selb[j:j + 1, :]

    init = (jnp.full((1, W), M_INIT, F32), jnp.zeros((1, W), F32), jnp.zeros((HEAD_DIM, W), F32))

    k_io = lax.broadcasted_iota(I32, (KC, 1), 0)
    win_chunks = [qi * (TQ // KC) - A_WINDOW // KC + j for j in range((A_WINDOW + TQ) // KC)]
    biases = []
    for c in win_chunks:
        kpos = c * KC + k_io
        biases.append(jnp.where((kpos <= qpos) & (kpos > qpos - A_WINDOW) & (kpos >= 0), 0.0, MASKED))
    clamped = [jnp.maximum(c, 0) for c in win_chunks]
    k_w = jnp.concatenate([kw_ref[0, 0, c] for c in clamped], axis=0)
    bias_w = _tile_lanes(jnp.concatenate(biases, axis=0), H)
    _, l_w, acc_w = _flash_step(k_w, [vwT_ref[0, 0, c] for c in clamped], qT, bias_w, *init)

    ks_io = lax.broadcasted_iota(I32, (KS, 1), 0)
    per_step = KS // SEL_LEN

    def sel_body(st, carry):
        kpos = st * KS + ks_io
        blk = jnp.concatenate(
            [jnp.broadcast_to(selb_ref[per_step * st + r], (SEL_LEN, TQ)) for r in range(per_step)], axis=0)
        bias = _tile_lanes(jnp.where(kpos <= qpos, blk, MASKED), H)
        return _flash_step(ks_ref[0, 0, st], [vsT_ref[0, 0, st]], qT, bias, *carry)

    _, l_s, acc_s = lax.fori_loop(0, (qi * TQ + TQ - 1) // KS + 1, sel_body, init)

    g16 = gT_ref[0]

    def gate(branch):
        return jnp.concatenate([g16[branch * H + h:branch * H + h + 1, :] for h in range(H)], axis=1)

    oT = gate(0) * o_c + gate(1) * (acc_s / l_s) + gate(2) * (acc_w / l_w)
    o_ref[0] = _heads_to_rows(oT, H).astype(BF16)


def _nsa_attn(qT, kc, vcT, ks, vsT, kw, vwT, gT, covT, seq):
    bsz, G, nq = qT.shape[:3]
    W = A_QPG * TQ
    ncmp = kc.shape[2]
    n_sel = covT.shape[0]
    per_bg = lambda b, g, i: (b, g, 0, 0)
    per_bg5 = lambda b, g, i: (b, g, 0, 0, 0)
    return pl.pallas_call(
        _nsa_attn_kernel,
        grid=(bsz, G, nq),
        in_specs=[pl.BlockSpec((1, 1, 1, HEAD_DIM, W), lambda b, g, i: (b, g, i, 0, 0)),
                  pl.BlockSpec((1, 1, ncmp, HEAD_DIM), per_bg),
                  pl.BlockSpec((1, 1, HEAD_DIM, ncmp), per_bg),
                  pl.BlockSpec((1, 1) + ks.shape[2:], per_bg5),
                  pl.BlockSpec((1, 1) + vsT.shape[2:], per_bg5),
                  pl.BlockSpec((1, 1) + kw.shape[2:], per_bg5),
                  pl.BlockSpec((1, 1) + vwT.shape[2:], per_bg5),
                  pl.BlockSpec((1, GATE_ROWS, TQ), lambda b, g, i: (b, g, i)),
                  pl.BlockSpec((n_sel, ncmp), lambda b, g, i: (0, 0))],
        out_specs=pl.BlockSpec((1, TQ, A_QPG * HEAD_DIM), lambda b, g, i: (b, i, g)),
        out_shape=jax.ShapeDtypeStruct((bsz, seq, D_MODEL), BF16),
        scratch_shapes=[pltpu.VMEM((n_sel, 1, TQ), F32)],
        compiler_params=_cparams("parallel", "parallel", "arbitrary"),
        name="nsa_attention",
    )(qT, kc, vcT, ks, vsT, kw, vwT, gT, covT)


def _swa_attn_kernel(qT_ref, k_ref, vT_ref, sink_ref, o_ref):
    qi = pl.program_id(2)
    H = B_QPG
    W = H * TQ
    qpos = qi * TQ + lax.broadcasted_iota(I32, (1, TQ), 1)
    k_io = lax.broadcasted_iota(I32, (KC, 1), 0)
    chunks = [qi * (TQ // KC) - B_WINDOW // KC + j for j in range((B_WINDOW + TQ) // KC)]
    biases = []
    for c in chunks:
        kpos = c * KC + k_io
        biases.append(jnp.where((kpos <= qpos) & (kpos > qpos - B_WINDOW) & (kpos >= 0), 0.0, MASKED))
    bias = _tile_lanes(jnp.concatenate(biases, axis=0), H)
    clamped = [jnp.maximum(c, 0) for c in chunks]
    k = jnp.concatenate([k_ref[0, 0, c] for c in clamped], axis=0)
    vTs = [vT_ref[0, 0, c] for c in clamped]
    _, l, acc = _flash_step(k, vTs, qT_ref[0, 0, 0], bias, sink_ref[0], jnp.ones((1, W), F32),
                            jnp.zeros((HEAD_DIM, W), F32))
    o_ref[0] = _heads_to_rows(acc / l, H).astype(BF16)


def _swa_attn(qT, k, vT, sink_rows, seq):
    bsz, G, nq = qT.shape[:3]
    W = B_QPG * TQ
    nch = k.shape[2]
    per_bg5 = lambda b, g, i: (b, g, 0, 0, 0)
    return pl.pallas_call(
        _swa_attn_kernel,
        grid=(bsz, G, nq),
        in_specs=[pl.BlockSpec((1, 1, 1, HEAD_DIM, W), lambda b, g, i: (b, g, i, 0, 0)),
                  pl.BlockSpec((1, 1, nch, KC, HEAD_DIM), per_bg5),
                  pl.BlockSpec((1, 1, nch, HEAD_DIM, KC), per_bg5),
                  pl.BlockSpec((1, 1, W), lambda b, g, i: (g, 0, 0))],
        out_specs=pl.BlockSpec((1, TQ, B_QPG * HEAD_DIM), lambda b, g, i: (b, i, g)),
        out_shape=jax.ShapeDtypeStruct((bsz, seq, D_MODEL), BF16),
        compiler_params=_cparams("parallel", "parallel", "arbitrary"),
        name="swa_attention",
    )(qT, k, vT, sink_rows)


def _swa_proj_kernel(x_ref, gkv_ref, sckv_ref, shkv_ref, gq_ref, scq_ref, shq_ref,
                     wkv_ref, bkv_ref, wq_ref, bq_ref, rc_ref, ra_ref, rb_ref, k_ref, vT_ref, qT_ref):
    x = x_ref[...]
    rc, ra, rb = rc_ref[...], ra_ref[...], rb_ref[...]
    hkv = _norm_mod(x, gkv_ref[...], sckv_ref[0], shkv_ref[0]).astype(BF16)
    kv = _dot(hkv, wkv_ref[...]) + bkv_ref[...]
    kw = B_KV_HEADS * HEAD_DIM
    _store_k(k_ref, _rope(kv[:, :kw], rc, ra, rb), B_KV_HEADS, KC)
    _store_vT(vT_ref, kv[:, kw:], B_KV_HEADS, KC)
    hq = _norm_mod(x, gq_ref[...], scq_ref[0], shq_ref[0]).astype(BF16)
    q = _dot(hq, wq_ref[...]) + bq_ref[...]
    _store_qT(qT_ref, _rope(q, rc, ra, rb) * QK_SCALE, B_KV_HEADS, B_QPG)


def _swa_proj(x, gkv, sckv, shkv, gq, scq, shq, wkv, bkv, wq, bq, rc, ra, rb, seq):
    T = x.shape[0]
    bsz = T // seq
    per_b = seq // TM
    G = B_KV_HEADS
    row = lambda i: (i, 0)
    fixed = lambda i: (0, 0)
    bat = lambda i: (i // per_b, 0, 0)
    tile5 = lambda i: (i // per_b, 0, i % per_b, 0, 0)
    vec = pl.BlockSpec((1, D_MODEL), fixed)
    mod = pl.BlockSpec((1, 1, D_MODEL), bat)
    tab = pl.BlockSpec((TM, LANES), row)
    return pl.pallas_call(
        _swa_proj_kernel,
        grid=(T // TM,),
        in_specs=[pl.BlockSpec((TM, D_MODEL), row), vec, mod, mod, vec, mod, mod,
                  pl.BlockSpec(wkv.shape, fixed), pl.BlockSpec(bkv.shape, fixed),
                  pl.BlockSpec(wq.shape, fixed), pl.BlockSpec(bq.shape, fixed), tab, tab, tab],
        out_specs=[pl.BlockSpec((1, G, TM // KC, KC, HEAD_DIM), tile5),
                   pl.BlockSpec((1, G, TM // KC, HEAD_DIM, KC), tile5),
                   pl.BlockSpec((1, G, TM // TQ, HEAD_DIM, B_QPG * TQ), tile5)],
        out_shape=[jax.ShapeDtypeStruct((bsz, G, seq // KC, KC, HEAD_DIM), BF16),
                   jax.ShapeDtypeStruct((bsz, G, seq // KC, HEAD_DIM, KC), BF16),
                   jax.ShapeDtypeStruct((bsz, G, seq // TQ, HEAD_DIM, B_QPG * TQ), BF16)],
        compiler_params=_cparams("parallel"),
        name="swa_proj",
    )(x, gkv, sckv, shkv, gq, scq, shq, wkv, bkv, wq, bq, rc, ra, rb)


def _oproj_kernel(x_ref, w_ref, b_ref, res_ref, gate_ref, o_ref):
    y = _dot(x_ref[...], w_ref[...]) + b_ref[...]
    o_ref[...] = res_ref[...] + gate_ref[0] * y


def _oproj(x, w, b, res, gate, seq):
    T = x.shape[0]
    per_b = seq // TM
    row = lambda i: (i, 0)
    fixed = lambda i: (0, 0)
    return pl.pallas_call(
        _oproj_kernel,
        grid=(T // TM,),
        in_specs=[pl.BlockSpec((TM, x.shape[1]), row), pl.BlockSpec(w.shape, fixed),
                  pl.BlockSpec((1, D_MODEL), fixed), pl.BlockSpec((TM, D_MODEL), row),
                  pl.BlockSpec((1, 1, D_MODEL), lambda i: (i // per_b, 0, 0))],
        out_specs=pl.BlockSpec((TM, D_MODEL), row),
        out_shape=jax.ShapeDtypeStruct((T, D_MODEL), F32),
        compiler_params=_cparams("parallel"),
        name="out_proj",
    )(x, w, b, res, gate)


def _router_kernel(x_ref, g_ref, sc_ref, sh_ref, wh_ref, wl_ref, b_ref, tri_ref, hn_ref, tw_ref, cnt_ref, run_ref):
    @pl.when(pl.program_id(0) == 0)
    def _():
        run_ref[...] = jnp.zeros(run_ref.shape, F32)

    hn = _norm_mod(x_ref[...], g_ref[...], sc_ref[0], sh_ref[0])
    _store_row_tiles(hn_ref, hn)
    hi = hn.astype(BF16)
    lo = (hn - hi.astype(F32)).astype(BF16)
    wh = wh_ref[...]
    logits = _dot(hi, wh) + _dot(lo, wh) + _dot(hi, wl_ref[...]) + b_ref[...]
    col = lax.broadcasted_iota(I32, logits.shape, 1)
    cur = jnp.where(col < N_EXPERTS, logits, -jnp.inf)
    vals, idxs = [], []
    for _ in range(TOP_K):
        mx = jnp.max(cur, axis=-1, keepdims=True)
        ix = jnp.min(jnp.where(cur == mx, col, LANES), axis=-1, keepdims=True)
        vals.append(mx)
        idxs.append(ix)
        cur = jnp.where(col == ix, -jnp.inf, cur)
    es = [jnp.exp(v - vals[0]) for v in vals]
    den = es[0] + es[1] + es[2] + es[3]

    hots = [col == ix for ix in idxs]
    osum = (hots[0] | hots[1] | hots[2] | hots[3]).astype(F32)
    before = _dot(tri_ref[...], osum.astype(BF16)) + run_ref[...]
    ranks = [jnp.sum(jnp.where(hot, before, 0.0), axis=-1, keepdims=True) for hot in hots]
    run_ref[...] = run_ref[...] + jnp.sum(osum, axis=0, keepdims=True)
    cnt_ref[...] = run_ref[...]

    packed = jnp.zeros(logits.shape, F32)
    for k in range(TOP_K):
        packed = jnp.where(col == k, idxs[k].astype(F32), packed)
        packed = jnp.where(col == TOP_K + k, es[k] / den, packed)
        packed = jnp.where(col == 2 * TOP_K + k, ranks[k], packed)
    tw_ref[...] = packed.T[:4 * TOP_K, :]


def _router(x, g, scale, shift, wh, wl, b, seq):
    T = x.shape[0]
    per_b = seq // TM
    row = lambda i: (i, 0)
    fixed = lambda i: (0, 0)
    bat = lambda i: (i // per_b, 0, 0)
    tri = jnp.asarray(np.tril(np.ones((TM, TM), np.float32), -1), BF16)
    return pl.pallas_call(
        _router_kernel,
        grid=(T // TM,),
        in_specs=[pl.BlockSpec((TM, D_MODEL), row), pl.BlockSpec((1, D_MODEL), fixed),
                  pl.BlockSpec((1, 1, D_MODEL), bat), pl.BlockSpec((1, 1, D_MODEL), bat),
                  pl.BlockSpec(wh.shape, fixed), pl.BlockSpec(wl.shape, fixed),
                  pl.BlockSpec((1, LANES), fixed), pl.BlockSpec((TM, TM), fixed)],
        out_specs=[pl.BlockSpec((TM * ROW_TILE, LANES), row), pl.BlockSpec((4 * TOP_K, TM), lambda i: (0, i)),
                   pl.BlockSpec((1, LANES), fixed)],
        out_shape=[jax.ShapeDtypeStruct((T * ROW_TILE, LANES), F32),
                   jax.ShapeDtypeStruct((4 * TOP_K, T), F32),
                   jax.ShapeDtypeStruct((1, LANES), F32)],
        scratch_shapes=[pltpu.VMEM((1, LANES), F32)],
        compiler_params=_cparams("arbitrary"),
        name="moe_router",
    )(x, g, scale, shift, wh, wl, b, tri)


def _dispatch_kernel(ids_ref, hn_ref, xs_in_ref, xs_ref, sem):
    del xs_in_ref

    def row_copy(r, slot):
        src = hn_ref.at[pl.ds(pl.multiple_of(r * ROW_TILE, ROW_TILE), ROW_TILE)]
        dst = xs_ref.at[pl.ds(pl.multiple_of(slot * ROW_TILE, ROW_TILE), ROW_TILE)]
        return pltpu.make_async_copy(src, dst, sem)

    def issue(r, carry):
        for k in range(TOP_K):
            row_copy(r, ids_ref[0, 0, k * TD + r]).start()
        return carry

    lax.fori_loop(0, TD, issue, 0, unroll=8)

    def drain(r, carry):
        for k in range(TOP_K):
            row_copy(r, ids_ref[0, 0, k * TD + r]).wait()
        return carry

    lax.fori_loop(0, TD, drain, 0, unroll=8)


def _dispatch(hn, slot_tiles, xs_zero):
    T = hn.shape[0] // ROW_TILE
    return pl.pallas_call(
        _dispatch_kernel,
        grid=(T // TD,),
        in_specs=[pl.BlockSpec((1, 1, TOP_K * TD), lambda i: (i, 0, 0), memory_space=pltpu.SMEM),
                  pl.BlockSpec((TD * ROW_TILE, LANES), lambda i: (i, 0)),
                  pl.BlockSpec(memory_space=pl.ANY)],
        out_specs=pl.BlockSpec(memory_space=pl.ANY),
        out_shape=jax.ShapeDtypeStruct(xs_zero.shape, xs_zero.dtype),
        input_output_aliases={2: 0},
        scratch_shapes=[pltpu.SemaphoreType.DMA(())],
        compiler_params=_cparams("arbitrary"),
        name="moe_dispatch",
    )(slot_tiles, hn, xs_zero)


def _experts_kernel(be_ref, nu_ref, x_ref, wgu_ref, bgu_ref, wd_ref, bd_ref, o_ref, wgu_s, wd_s):
    i = pl.program_id(0)
    used = i < nu_ref[0]
    new_expert = (i == 0) | (be_ref[i] != be_ref[jnp.maximum(i - 1, 0)])

    @pl.when(used & new_expert)
    def _():
        wgu_s[...] = wgu_ref[0].astype(BF16)
        wd_s[...] = wd_ref[0].astype(BF16)

    @pl.when(used)
    def _():
        x = _load_row_tiles(x_ref, MOE_BLOCK).astype(BF16)
        gu = _dot(x, wgu_s[...]) + bgu_ref[0]
        glu = jnp.minimum(gu[:, :D_FF], SWIGLU_LIMIT)
        lin = jnp.clip(gu[:, D_FF:], -SWIGLU_LIMIT, SWIGLU_LIMIT)
        act = glu * jax.nn.sigmoid(SWIGLU_ALPHA * glu) * (lin + 1.0)
        _store_row_tiles(o_ref, _dot(act.astype(BF16), wd_s[...]) + bd_ref[0])

    @pl.when(i >= nu_ref[0])
    def _():
        o_ref[...] = jnp.zeros(o_ref.shape, o_ref.dtype)


def _experts(blk_expert, n_used, xs, wgu, bgu, wd, bd, layer):
    n_slots = xs.shape[0] // ROW_TILE
    n_blk = n_slots // MOE_BLOCK
    row = lambda i, be, nu: (i, 0)
    ex3 = lambda i, be, nu: (layer * N_EXPERTS + be[i], 0, 0)
    grid_spec = pltpu.PrefetchScalarGridSpec(
        num_scalar_prefetch=2,
        grid=(n_blk,),
        in_specs=[pl.BlockSpec((MOE_BLOCK * ROW_TILE, LANES), row),
                  pl.BlockSpec((1, D_MODEL, 2 * D_FF), ex3),
                  pl.BlockSpec((1, 1, 2 * D_FF), ex3),
                  pl.BlockSpec((1, D_FF, D_MODEL), ex3),
                  pl.BlockSpec((1, 1, D_MODEL), ex3)],
        out_specs=pl.BlockSpec((MOE_BLOCK * ROW_TILE, LANES), row),
        scratch_shapes=[pltpu.VMEM((D_MODEL, 2 * D_FF), BF16), pltpu.VMEM((D_FF, D_MODEL), BF16)])
    return pl.pallas_call(
        _experts_kernel,
        grid_spec=grid_spec,
        out_shape=jax.ShapeDtypeStruct((n_slots * ROW_TILE, LANES), F32),
        compiler_params=pltpu.CompilerParams(dimension_semantics=("arbitrary",),
                                             vmem_limit_bytes=EXPERTS_VMEM_LIMIT),
        name="moe_experts",
    )(blk_expert, n_used, xs, wgu.reshape(-1, D_MODEL, 2 * D_FF), bgu.reshape(-1, 1, 2 * D_FF),
      wd.reshape(-1, D_FF, D_MODEL), bd.reshape(-1, 1, D_MODEL))


def _combine_kernel(ids_ref, nxt_ref, outs_ref, w_ref, res_ref, gate_ref, fg_ref, out_ref, buf, sem, *, final_norm):
    i = pl.program_id(0)
    n = pl.num_programs(0)

    def row_copy(ids, r, k, par):
        src = outs_ref.at[pl.ds(pl.multiple_of(ids[0, 0, k * TC + r] * ROW_TILE, ROW_TILE), ROW_TILE)]
        dst = buf.at[par, k, pl.ds(pl.multiple_of(r * ROW_TILE, ROW_TILE), ROW_TILE)]
        return pltpu.make_async_copy(src, dst, sem.at[par])

    def issue(ids, par):
        def body(r, carry):
            for k in range(TOP_K):
                row_copy(ids, r, k, par).start()
            return carry
        lax.fori_loop(0, TC, body, 0, unroll=8)

    @pl.when(i == 0)
    def _():
        issue(ids_ref, 0)

    @pl.when(i + 1 < n)
    def _():
        issue(nxt_ref, (i + 1) % 2)

    par = i % 2

    def drain(r, carry):
        for k in range(TOP_K):
            row_copy(ids_ref, r, k, par).wait()
        return carry

    lax.fori_loop(0, TC, drain, 0, unroll=8)

    w = w_ref[...]
    y = w[:, 0:1] * _load_row_tiles(buf.at[par, 0], TC)
    for k in range(1, TOP_K):
        y = y + w[:, k:k + 1] * _load_row_tiles(buf.at[par, k], TC)
    h = res_ref[...] + gate_ref[0] * y
    if final_norm:
        h = h * lax.rsqrt(jnp.mean(h * h, axis=-1, keepdims=True) + NORM_EPS) * fg_ref[...]
    out_ref[...] = h


def _combine(outs, slot_tiles, wts, res, gate, fg, seq, final_norm):
    T = res.shape[0]
    per_b = seq // TC
    nblk = T // TC
    row = lambda i: (i, 0)
    fixed = lambda i: (0, 0)
    ids = lambda i: (i, 0, 0)
    nxt = lambda i: (jnp.minimum(i + 1, nblk - 1), 0, 0)
    return pl.pallas_call(
        functools.partial(_combine_kernel, final_norm=final_norm),
        grid=(nblk,),
        in_specs=[pl.BlockSpec((1, 1, TOP_K * TC), ids, memory_space=pltpu.SMEM),
                  pl.BlockSpec((1, 1, TOP_K * TC), nxt, memory_space=pltpu.SMEM),
                  pl.BlockSpec(memory_space=pl.ANY),
                  pl.BlockSpec((TC, TOP_K), row), pl.BlockSpec((TC, D_MODEL), row),
                  pl.BlockSpec((1, 1, D_MODEL), lambda i: (i // per_b, 0, 0)),
                  pl.BlockSpec((1, D_MODEL), fixed)],
        out_specs=pl.BlockSpec((TC, D_MODEL), row),
        out_shape=jax.ShapeDtypeStruct((T, D_MODEL), F32),
        scratch_shapes=[pltpu.VMEM((2, TOP_K, TC * ROW_TILE, LANES), F32), pltpu.SemaphoreType.DMA((2,))],
        compiler_params=_cparams("arbitrary"),
        name="moe_combine",
    )(slot_tiles, slot_tiles, outs, wts, res, gate, fg)


def _rope_tables(positions):
    inv = ROPE_THETA ** (-jnp.arange(0, 2 * ROT_HALF, 2, dtype=F32) / (2 * ROT_HALF))
    ang = positions.astype(F32)[..., None] * inv
    cos, sin = jnp.cos(ang), jnp.sin(ang)
    lead = cos.shape[:-1]
    ones = jnp.ones(lead + (HEAD_DIM - 2 * ROT_HALF,), F32)
    zeros = jnp.zeros(lead + (HEAD_DIM - 2 * ROT_HALF,), F32)
    z8 = jnp.zeros(lead + (ROT_HALF,), F32)
    rc = jnp.concatenate([cos, cos, ones], -1)
    ra = jnp.concatenate([-sin, z8, zeros], -1)
    rb = jnp.concatenate([z8, sin, zeros], -1)
    two = lambda t: jnp.concatenate([t, t], -1)
    return two(rc), two(ra), two(rb)


def _moe_slots(expert, rank, counts, n_tok):
    pad_counts = (counts + MOE_BLOCK - 1) // MOE_BLOCK * MOE_BLOCK
    pad_end = jnp.cumsum(pad_counts)
    pad_start = pad_end - pad_counts
    hot = expert[..., None] == jnp.arange(N_EXPERTS, dtype=I32)
    slot = jnp.sum(jnp.where(hot, pad_start.astype(I32), 0), axis=-1) + rank
    n_blk = (n_tok * TOP_K + N_EXPERTS * MOE_BLOCK) // MOE_BLOCK
    blk_start = jnp.arange(n_blk) * MOE_BLOCK
    blk_expert = jnp.minimum(jnp.sum(pad_end[None, :] <= blk_start[:, None], axis=1), N_EXPERTS - 1).astype(I32)
    n_used = (pad_end[-1] // MOE_BLOCK).astype(I32).reshape(1)
    return slot.astype(I32), blk_expert, n_used


def _slot_tiles(slot, tile):
    n_tok = slot.shape[1]
    return slot.reshape(TOP_K, n_tok // tile, tile).transpose(1, 0, 2).reshape(n_tok // tile, 1, TOP_K * tile)


def _moe_layer(h, g, scale, shift, gate, router_w, router_b, wgu, bgu, wd, bd, fg, seq, layer, final_norm):
    T = h.shape[0]
    rw = jnp.pad(router_w, ((0, 0), (0, LANES - N_EXPERTS)))
    rwh = rw.astype(BF16)
    rwl = (rw - rwh.astype(F32)).astype(BF16)
    rb = jnp.pad(router_b, (0, LANES - N_EXPERTS)).reshape(1, LANES)
    hn, tw, cnt = _router(h, g, scale, shift, rwh, rwl, rb, seq)
    expert = tw[:TOP_K].astype(I32)
    wts = tw[TOP_K:2 * TOP_K].T
    rank = tw[2 * TOP_K:3 * TOP_K].astype(I32)
    slot, blk_expert, n_used = _moe_slots(expert, rank, cnt[0, :N_EXPERTS].astype(I32), T)
    n_slots = T * TOP_K + N_EXPERTS * MOE_BLOCK
    xs = _dispatch(hn, _slot_tiles(slot, TD), jnp.zeros((n_slots * ROW_TILE, LANES), F32))
    outs = _experts(blk_expert, n_used, xs, wgu, bgu, wd, bd, layer)
    return _combine(outs, _slot_tiles(slot, TC), wts, h, gate, fg, seq, final_norm)


def _cover_matrix(seq):
    n_cmp = (seq - CMP_LEN) // CMP_STRIDE + 1
    n_sel = seq // SEL_LEN
    cmp_lo = np.arange(n_cmp) * CMP_STRIDE
    sel_lo = np.arange(n_sel) * SEL_LEN
    cover = ((cmp_lo[:, None] < sel_lo[None, :] + SEL_LEN)
             & (sel_lo[None, :] < cmp_lo[:, None] + CMP_LEN)).astype(np.float32)
    covT = np.zeros((n_sel, seq // CMP_STRIDE), np.float32)
    covT[:, :n_cmp] = cover.T
    return jnp.asarray(covT, BF16)


def kernel(x, c, positions, mod_w, mod_b, norm_g, final_norm_g, a_w_in, a_cmp_pe, a_cmp_w1, a_cmp_w2,
           a_w_out, kv_norm_g, kv_mod_w, kv_mod_b, kv_w, kv_b, b_w_q, b_b_q, b_sinks, b_w_out, b_b_out,
           router_w, router_b, w_gate_up, b_gate_up, w_down, b_down):
    bsz, seq, _ = x.shape
    T = bsz * seq
    G = A_GROUPS
    hd = A_GROUPS * A_QPG * HEAD_DIM
    kd = A_GROUPS * HEAD_DIM

    mods = _mods(c, mod_w.reshape(4, D_MODEL, 3 * D_MODEL), mod_b.reshape(4, 3 * D_MODEL))
    kv_mods = _mods(c, kv_mod_w[None], kv_mod_b[None])[0]

    def split_mod(m, n):
        return [p.reshape(bsz, 1, D_MODEL) for p in jnp.split(m, n, axis=-1)]

    rc, ra, rb = _rope_tables(positions)
    rc2, ra2, rb2 = (t.reshape(T, LANES) for t in (rc, ra, rb))
    h = x.reshape(T, D_MODEL)
    vec = lambda v: v.reshape(1, D_MODEL)

    shift, scale, gate = split_mod(mods[0], 3)
    w_in = a_w_in[0]
    wq = w_in[:, :hd].astype(BF16)
    wkv = w_in[:, hd:hd + 6 * kd].astype(BF16)
    ggrp, ghead, gbr = np.meshgrid(np.arange(G), np.arange(A_QPG), np.arange(3), indexing="ij")
    gate_cols = (ggrp * GATE_ROWS + gbr * A_QPG + ghead).reshape(-1)
    wg = jnp.zeros((D_MODEL, LANES), F32).at[:, gate_cols].set(w_in[:, hd + 6 * kd:]).astype(BF16)
    qT, ckv, ks, vsT, kw, vwT, gT = _nsa_proj(h, vec(norm_g[0, 0]), scale, shift, wq, wkv, wg,
                                              rc2, ra2, rb2, seq)
    craw = ckv.reshape(bsz, seq, 2, G, HEAD_DIM).transpose(2, 0, 3, 1, 4).reshape(
        2, bsz, G, seq // CMP_STRIDE, CMP_STRIDE * HEAD_DIM)

    ncmp = seq // CMP_STRIDE
    end_pos = jnp.minimum(jnp.arange(ncmp) * CMP_STRIDE + CMP_LEN - 1, seq - 1)
    crc, cra, crb = _rope_tables(positions[:, end_pos])
    ident = (jnp.ones_like(crc), jnp.zeros_like(cra), jnp.zeros_like(crb))
    ctab = [jnp.stack([t, i]) for t, i in zip((crc, cra, crb), ident)]
    pe = a_cmp_pe[0].reshape(2, 2, CMP_STRIDE * HEAD_DIM)
    w1 = a_cmp_w1[0].reshape(2, 2, CMP_STRIDE * HEAD_DIM, CMP_HID).astype(BF16)
    w2 = jnp.pad(a_cmp_w2[0], ((0, 0), (0, 0), (0, LANES - HEAD_DIM))).astype(BF16)
    kvc = _compress(craw, pe, w1, w2, *ctab)
    kc = kvc[0]
    vcT = kvc[1].transpose(0, 1, 3, 2)

    o = _nsa_attn(qT, kc, vcT, ks, vsT, kw, vwT, gT, _cover_matrix(seq), seq)
    h = _oproj(o.reshape(T, D_MODEL), a_w_out[0].astype(BF16), jnp.zeros((1, D_MODEL), F32), h, gate, seq)

    shift, scale, gate = split_mod(mods[1], 3)
    h = _moe_layer(h, vec(norm_g[0, 1]), scale, shift, gate, router_w[0], router_b[0], w_gate_up,
                   b_gate_up, w_down, b_down, vec(final_norm_g), seq, 0, False)

    kv_shift, kv_scale = split_mod(kv_mods, 2)
    shift, scale, gate = split_mod(mods[2], 3)
    kb, vbT, qT = _swa_proj(h, vec(kv_norm_g), kv_scale, kv_shift, vec(norm_g[1, 0]), scale, shift,
                            kv_w.astype(BF16), kv_b.reshape(1, -1), b_w_q[0].astype(BF16),
                            b_b_q[0].reshape(1, -1), rc2, ra2, rb2, seq)
    sink_rows = jnp.repeat(b_sinks[0].reshape(B_KV_HEADS, 1, B_QPG), TQ, axis=-1)
    o = _swa_attn(qT, kb, vbT, sink_rows, seq)
    h = _oproj(o.reshape(T, D_MODEL), b_w_out[0].astype(BF16), b_b_out[0].reshape(1, D_MODEL), h, gate, seq)

    shift, scale, gate = split_mod(mods[3], 3)
    out = _moe_layer(h, vec(norm_g[1, 1]), scale, shift, gate, router_w[1], router_b[1], w_gate_up,
                     b_gate_up, w_down, b_down, vec(final_norm_g), seq, 1, True)
    return out.reshape(bsz, seq, D_MODEL)
```
